```python
import math
import jax, jax.numpy as jnp
from jax import lax
import numpy as np

D_MODEL = 2048
BATCH = 8
SEQ = 2048
DEPTH = 4

GRID_W = 64

MLA_HEADS = 8
MLA_Q_LORA = D_MODEL // 4
MLA_KV_LORA = D_MODEL // 8
MLA_NOPE = 128
MLA_ROPE = 64
MLA_V = (D_MODEL // 2) // MLA_HEADS
MLA_WIDTH = MLA_HEADS * MLA_V
ROPE_BASE = 10000.0
Q_BLOCK = 128

NAT_HEAD_DIM = 64
NAT_HEADS = (D_MODEL // 2) // NAT_HEAD_DIM
NAT_WIDTH = NAT_HEADS * NAT_HEAD_DIM
NAT_WIN_R = 8
NAT_WIN_C = 16

D_MIX = MLA_WIDTH + NAT_WIDTH

IN_WIDTHS = (MLA_Q_LORA, MLA_KV_LORA, MLA_ROPE, NAT_WIDTH, NAT_WIDTH, NAT_WIDTH)
D_IN = sum(IN_WIDTHS)
IN_OFFSETS = [int(o) for o in np.cumsum(IN_WIDTHS)[:-1]]

PEER_HEADS = 8
PEER_N_KEYS = 128
PEER_N_EXPERTS = PEER_N_KEYS * PEER_N_KEYS
PEER_TOPK = 16
PEER_D_QUERY = 256
PEER_TOKEN_BLOCK = 128

RMS_EPS = 1e-6

kernel_name = "hymba_mla_natten_peer_encoder"


def rms_norm(x, g):
    xf = x.astype(jnp.float32)
    y = xf * lax.rsqrt(jnp.mean(xf * xf, axis=-1, keepdims=True) + RMS_EPS)
    return (y * g.astype(jnp.float32)).astype(x.dtype)


def rope_tables(seq_len):
    inv = ROPE_BASE ** (-jnp.arange(0, MLA_ROPE, 2, dtype=jnp.float32) / MLA_ROPE)
    ang = jnp.arange(seq_len, dtype=jnp.float32)[:, None] * inv[None, :]
    return jnp.cos(ang), jnp.sin(ang)


def apply_rope(t, cos, sin):
    tf = t.astype(jnp.float32)
    half = tf.shape[-1] // 2
    t1, t2 = tf[..., :half], tf[..., half:]
    return jnp.concatenate([t1 * cos - t2 * sin, t2 * cos + t1 * sin], axis=-1).astype(t.dtype)


def mla_group(c_q, c_kv, k_pe, q_norm, w_uq, kv_norm, w_ukv):
    B, S, _ = c_q.shape
    cos, sin = rope_tables(S)
    q = (rms_norm(c_q, q_norm) @ w_uq).reshape(B, S, MLA_HEADS, MLA_NOPE + MLA_ROPE)
    q_nope = q[..., :MLA_NOPE]
    q_pe = apply_rope(q[..., MLA_NOPE:], cos[:, None, :], sin[:, None, :])
    k_pe = apply_rope(k_pe, cos, sin)
    kv = (rms_norm(c_kv, kv_norm) @ w_ukv).reshape(B, S, MLA_HEADS, MLA_NOPE + MLA_V)
    k_nope, v = kv[..., :MLA_NOPE], kv[..., MLA_NOPE:]
    scale = (MLA_NOPE + MLA_ROPE) ** -0.5
    n_blk = S // Q_BLOCK

    def to_blocks(t):
        return jnp.moveaxis(t.reshape(B, n_blk, Q_BLOCK, *t.shape[2:]), 1, 0)

    def attend(blk):
        qn, qp = blk
        s = (jnp.einsum('bqhd,bkhd->bhqk', qn, k_nope, preferred_element_type=jnp.float32)
             + jnp.einsum('bqhd,bkd->bhqk', qp, k_pe, preferred_element_type=jnp.float32))
        p = jax.nn.softmax(s * scale, axis=-1).astype(v.dtype)
        return jnp.einsum('bhqk,bkhd->bqhd', p, v)

    o = lax.map(attend, (to_blocks(q_nope), to_blocks(q_pe)))
    return jnp.moveaxis(o, 0, 1).reshape(B, S, MLA_WIDTH)


def nat_group(q, k, v, rpb):
    B, S, _ = q.shape
    rows = S // GRID_W
    kr = min(NAT_WIN_R, rows)
    n_cb = GRID_W // NAT_WIN_C
    band_w = 2 * NAT_WIN_C

    def grid(t):
        return t.reshape(B, rows, GRID_W, NAT_HEADS, NAT_HEAD_DIM)

    q = grid(q * (NAT_HEAD_DIM ** -0.5))
    k = grid(k)
    v = grid(v)
    cb = np.clip(np.arange(n_cb) * NAT_WIN_C - NAT_WIN_C // 2, 0, GRID_W - band_w)
    key_cols = cb[:, None] + np.arange(band_w)
    q_cols = np.arange(n_cb)[:, None] * NAT_WIN_C + np.arange(NAT_WIN_C)
    c_start = np.clip(q_cols - NAT_WIN_C // 2, 0, GRID_W - NAT_WIN_C)
    kc = key_cols[:, None, :]
    col_mask = (kc >= c_start[..., None]) & (kc < c_start[..., None] + NAT_WIN_C)
    dc_idx = np.clip(kc - q_cols[..., None], -(NAT_WIN_C - 1), NAT_WIN_C - 1) + (NAT_WIN_C - 1)
    mask = jnp.asarray(col_mask)[:, :, None, :]

    def attend_row(r):
        rs = jnp.clip(r - kr // 2, 0, rows - kr)
        q_row = lax.dynamic_index_in_dim(q, r, axis=1, keepdims=False)
        q_row = q_row.reshape(B, n_cb, NAT_WIN_C, NAT_HEADS, NAT_HEAD_DIM)
        k_band = lax.dynamic_slice_in_dim(k, rs, kr, axis=1)[:, :, key_cols]
        v_band = lax.dynamic_slice_in_dim(v, rs, kr, axis=1)[:, :, key_cols]
        dr_idx = rs + jnp.arange(kr) - r + (NAT_WIN_R - 1)
        bias = rpb[:, dr_idx][:, :, dc_idx]
        bias = jnp.transpose(bias, (0, 2, 3, 1, 4)).astype(jnp.float32)
        s = jnp.einsum('bnqhd,binkhd->bhnqik', q_row, k_band,
                       preferred_element_type=jnp.float32) + bias
        s = jnp.where(mask, s, -jnp.inf)
        p = jax.nn.softmax(s.reshape(*s.shape[:4], kr * band_w), axis=-1)
        p = p.reshape(s.shape).astype(v.dtype)
        o = jnp.einsum('bhnqik,binkhd->bnqhd', p, v_band)
        return o.reshape(B, GRID_W, NAT_WIDTH)

    o = lax.map(attend_row, jnp.arange(rows))
    return jnp.moveaxis(o, 0, 1).reshape(B, S, NAT_WIDTH)


def peer_ffn(h, w_q, sub_keys, u, v):
    B, S, D = h.shape
    half = PEER_D_QUERY // 2
    q = (h @ w_q).reshape(B, S, PEER_HEADS, PEER_D_QUERY)
    s1 = jnp.einsum('bshd,nd->bshn', q[..., :half], sub_keys[0], preferred_element_type=jnp.float32)
    s2 = jnp.einsum('bshd,nd->bshn', q[..., half:], sub_keys[1], preferred_element_type=jnp.float32)
    v1, i1 = lax.top_k(s1, PEER_TOPK)
    v2, i2 = lax.top_k(s2, PEER_TOPK)
    cand_s = (v1[..., :, None] + v2[..., None, :]).reshape(B, S, PEER_HEADS, PEER_TOPK * PEER_TOPK)
    cand_i = (i1[..., :, None] * PEER_N_KEYS + i2[..., None, :]).reshape(B, S, PEER_HEADS, PEER_TOPK * PEER_TOPK)
    top_s, pos = lax.top_k(cand_s, PEER_TOPK)
    idx = jnp.take_along_axis(cand_i, pos, axis=-1)
    g = jax.nn.softmax(top_s, axis=-1)
    T = B * S
    K = PEER_HEADS * PEER_TOPK
    n_blk = T // PEER_TOKEN_BLOCK
    hb = h.reshape(n_blk, PEER_TOKEN_BLOCK, D)
    ib = idx.reshape(n_blk, PEER_TOKEN_BLOCK, K)
    gb = g.reshape(n_blk, PEER_TOKEN_BLOCK, K)

    def experts(blk):
        h_t, i_t, g_t = blk
        act = jnp.einsum('td,tkd->tk', h_t, u[i_t], preferred_element_type=jnp.float32)
        a = (jax.nn.gelu(act, approximate=False) * g_t).astype(h_t.dtype)
        return jnp.einsum('tk,tkd->td', a, v[i_t])

    y = lax.map(experts, (hb, ib, gb))
    return y.reshape(B, S, D)


def setup_inputs(seed: int = 0) -> dict:
    key = jax.random.key(seed)
    ks = jax.random.split(key, 17)
    L, D = DEPTH, D_MODEL
    f32 = jnp.float32

    def nrm(k, shape, scale):
        return jax.random.normal(k, shape, f32) * scale

    def gain(k, shape):
        return 1.0 + 0.02 * jax.random.normal(k, shape, f32)

    return {
        "x": jax.random.normal(ks[0], (BATCH, SEQ, D), f32),
        "attn_norm": gain(ks[1], (L, D)),
        "w_in": nrm(ks[2], (L, D, D_IN), D ** -0.5),
        "mla_q_norm": gain(ks[3], (L, MLA_Q_LORA)),
        "mla_w_uq": nrm(ks[4], (L, MLA_Q_LORA, MLA_HEADS * (MLA_NOPE + MLA_ROPE)), MLA_Q_LORA ** -0.5),
        "mla_kv_norm": gain(ks[5], (L, MLA_KV_LORA)),
        "mla_w_ukv": nrm(ks[6], (L, MLA_KV_LORA, MLA_HEADS * (MLA_NOPE + MLA_V)), MLA_KV_LORA ** -0.5),
        "nat_rpb": nrm(ks[7], (L, NAT_HEADS, 2 * NAT_WIN_R - 1, 2 * NAT_WIN_C - 1), 0.1),
        "mla_out_norm": gain(ks[8], (L, MLA_WIDTH)),
        "nat_out_norm": gain(ks[9], (L, NAT_WIDTH)),
        "w_out": nrm(ks[10], (L, D_MIX, D), D_MIX ** -0.5),
        "ffn_norm": gain(ks[11], (L, D)),
        "peer_w_q": nrm(ks[12], (L, D, PEER_HEADS * PEER_D_QUERY), D ** -0.5),
        "peer_sub_keys": nrm(ks[13], (L, 2, PEER_N_KEYS, PEER_D_QUERY // 2), (PEER_D_QUERY // 2) ** -0.5),
        "peer_u": nrm(ks[14], (L, PEER_N_EXPERTS, D), D ** -0.5),
        "peer_v": nrm(ks[15], (L, PEER_N_EXPERTS, D), PEER_HEADS ** -0.5),
        "final_norm": gain(ks[16], (D,)),
    }


def reference(x, attn_norm, w_in, mla_q_norm, mla_w_uq, mla_kv_norm, mla_w_ukv, nat_rpb,
              mla_out_norm, nat_out_norm, w_out, ffn_norm, peer_w_q, peer_sub_keys,
              peer_u, peer_v, final_norm):
    for l in range(DEPTH):
        h = rms_norm(x, attn_norm[l])
        proj = h @ w_in[l]
        c_q, c_kv, k_pe, nq, nk, nv = jnp.split(proj, IN_OFFSETS, axis=-1)
        mla_o = mla_group(c_q, c_kv, k_pe, mla_q_norm[l], mla_w_uq[l], mla_kv_norm[l], mla_w_ukv[l])
        nat_o = nat_group(nq, nk, nv, nat_rpb[l])
        mixed = jnp.concatenate([rms_norm(mla_o, mla_out_norm[l]),
                                 rms_norm(nat_o, nat_out_norm[l])], axis=-1)
        x = x + mixed @ w_out[l]
        x = x + peer_ffn(rms_norm(x, ffn_norm[l]), peer_w_q[l], peer_sub_keys[l], peer_u[l], peer_v[l])
    return rms_norm(x, final_norm)
```

```python
import functools

import numpy as np
import jax
import jax.numpy as jnp
from jax import lax
from jax.experimental import pallas as pl
from jax.experimental.pallas import tpu as pltpu

F32 = jnp.float32
BF16 = jnp.bfloat16

RMS_EPS = 1e-6
GRID_W = 64
MLA_HEADS = 8
MLA_NOPE = 128
MLA_ROPE = 64
ROPE_BASE = 10000.0
NAT_HEAD_DIM = 64
NAT_WIN_R = 8
NAT_WIN_C = 16
PEER_HEADS = 8
PEER_TOPK = 16
LANES = 128
NEG_BIG = -1e30
MIB = 1024 * 1024


def _params(sem, vmem_mib):
    return pltpu.CompilerParams(dimension_semantics=sem, vmem_limit_bytes=vmem_mib * MIB)


def _rms(x, g):
    ms = jnp.mean(x * x, axis=-1, keepdims=True)
    return x * lax.rsqrt(ms + RMS_EPS) * g


def _rmsnorm_kernel(x_ref, g_ref, o_ref, *, transpose):
    y = _rms(x_ref[...].astype(F32), g_ref[...])
    if transpose:
        y = y.T
    o_ref[...] = y.astype(o_ref.dtype)


def _rmsnorm(x, g, *, out_dtype, transpose=False, tm=512):
    t, d = x.shape
    tm = min(tm, t)
    if transpose:
        out_shape = jax.ShapeDtypeStruct((d, t), out_dtype)
        out_spec = pl.BlockSpec((d, tm), lambda i: (0, i))
    else:
        out_shape = jax.ShapeDtypeStruct((t, d), out_dtype)
        out_spec = pl.BlockSpec((tm, d), lambda i: (i, 0))
    return pl.pallas_call(
        functools.partial(_rmsnorm_kernel, transpose=transpose),
        grid=(t // tm,),
        in_specs=[pl.BlockSpec((tm, d), lambda i: (i, 0)), pl.BlockSpec((1, d), lambda i: (0, 0))],
        out_specs=out_spec,
        out_shape=out_shape,
        compiler_params=_params(("parallel",), 40),
        name="rmsnorm_t" if transpose else "rmsnorm",
    )(x, g.reshape(1, d).astype(F32))


def _mm_kernel(*refs, n_a, norm, has_res):
    a_refs = refs[:n_a]
    pos = n_a
    g_ref = refs[pos] if norm else None
    pos += int(norm)
    w_ref = refs[pos]
    pos += 1
    res_ref = refs[pos] if has_res else None
    pos += int(has_res)
    o_ref, an_ref = refs[pos], refs[pos + 1]

    @pl.when(pl.program_id(1) == 0)
    def _():
        parts = [r[...] for r in a_refs]
        a = parts[0] if n_a == 1 else jnp.concatenate(parts, axis=1)
        if norm:
            a = _rms(a.astype(F32), g_ref[...])
        an_ref[...] = a.astype(BF16)

    acc = jnp.dot(an_ref[...], w_ref[...], preferred_element_type=F32)
    if has_res:
        acc = acc + res_ref[...]
    o_ref[...] = acc.astype(o_ref.dtype)


def _matmul(a_list, w, *, out_dtype, gain=None, res=None, tm=512, tn=512, name="matmul"):
    t = a_list[0][0].shape[0]
    k, n = w.shape
    assert k == sum(width for _, _, width in a_list)
    tm, tn = min(tm, t), min(tn, n)
    in_specs, args = [], []
    for arr, cb, width in a_list:
        in_specs.append(pl.BlockSpec((tm, width), lambda i, j, cb=cb: (i, cb)))
        args.append(arr)
    if gain is not None:
        in_specs.append(pl.BlockSpec((1, k), lambda i, j: (0, 0)))
        args.append(gain.reshape(1, k).astype(F32))
    in_specs.append(pl.BlockSpec((k, tn), lambda i, j: (0, j)))
    args.append(w)
    if res is not None:
        in_specs.append(pl.BlockSpec((tm, tn), lambda i, j: (i, j)))
        args.append(res)
    return pl.pallas_call(
        functools.partial(_mm_kernel, n_a=len(a_list), norm=gain is not None, has_res=res is not None),
        grid=(t // tm, n // tn),
        in_specs=in_specs,
        out_specs=pl.BlockSpec((tm, tn), lambda i, j: (i, j)),
        out_shape=jax.ShapeDtypeStruct((t, n), out_dtype),
        scratch_shapes=[pltpu.VMEM((tm, k), BF16)],
        compiler_params=_params(("parallel", "arbitrary"), 48),
        name=name,
    )(*args)


def _mla_kernel(q_ref, k_ref, v_ref, ka_ref, kb_ref, cck_ref, ssk_ref, ccq_ref, ssq_ref, g_ref, o_ref, kcat_ref,
                *, heads, scale):
    @pl.when(pl.program_id(1) == 0)
    def _():
        rk = ka_ref[...].astype(F32) * cck_ref[...] + kb_ref[...].astype(F32) * ssk_ref[...]
        rk = rk.astype(BF16)
        for h in range(heads):
            kcat_ref[h, :, 0:LANES] = k_ref[:, h * LANES:(h + 1) * LANES]
            kcat_ref[h, :, LANES:2 * LANES] = rk

    tq = q_ref.shape[0]
    pe0 = heads * MLA_NOPE
    sw0 = pe0 + (heads // 2) * LANES
    ccq, ssq = ccq_ref[...], ssq_ref[...]
    lane = lax.broadcasted_iota(jnp.int32, (tq, LANES), 1)
    outs = []
    for h in range(heads):
        p_lo = pe0 + (h // 2) * LANES
        s_lo = sw0 + (h // 2) * LANES
        rq = q_ref[:, p_lo:p_lo + LANES].astype(F32) * ccq + q_ref[:, s_lo:s_lo + LANES].astype(F32) * ssq
        keep = (lane < MLA_ROPE) if h % 2 == 0 else (lane >= MLA_ROPE)
        rq = jnp.where(keep, rq, 0.0).astype(BF16)
        qcat = jnp.concatenate([q_ref[:, h * LANES:(h + 1) * LANES], rq], axis=1)
        s = lax.dot_general(qcat, kcat_ref[h], (((1,), (1,)), ((), ())), preferred_element_type=F32) * scale
        m = jnp.max(s, axis=-1, keepdims=True)
        p = jnp.exp(s - m)
        l = jnp.sum(p, axis=-1, keepdims=True)
        o = jnp.dot(p.astype(BF16), v_ref[:, h * LANES:(h + 1) * LANES], preferred_element_type=F32)
        outs.append(o / l)
    o_all = jnp.concatenate(outs, axis=1)
    o_ref[...] = _rms(o_all, g_ref[...]).astype(o_ref.dtype)


def _mla(q, kv, proj, cc, ss, gain, *, batch, seq, kpe_col, tq=256):
    heads = MLA_HEADS
    t = q.shape[0]
    tq = min(tq, seq)
    nq = seq // tq
    width = heads * LANES
    return pl.pallas_call(
        functools.partial(_mla_kernel, heads=heads, scale=float((MLA_NOPE + MLA_ROPE) ** -0.5)),
        grid=(batch, nq),
        in_specs=[
            pl.BlockSpec((tq, q.shape[1]), lambda b, i: (b * nq + i, 0)),
            pl.BlockSpec((seq, width), lambda b, i: (b, 0)),
            pl.BlockSpec((seq, width), lambda b, i: (b, 1)),
            pl.BlockSpec((seq, LANES), lambda b, i: (b, kpe_col)),
            pl.BlockSpec((seq, LANES), lambda b, i: (b, kpe_col + 1)),
            pl.BlockSpec((seq, LANES), lambda b, i: (0, 0)),
            pl.BlockSpec((seq, LANES), lambda b, i: (0, 0)),
            pl.BlockSpec((tq, LANES), lambda b, i: (i, 0)),
            pl.BlockSpec((tq, LANES), lambda b, i: (i, 0)),
            pl.BlockSpec((1, width), lambda b, i: (0, 0)),
        ],
        out_specs=pl.BlockSpec((tq, width), lambda b, i: (b * nq + i, 0)),
        out_shape=jax.ShapeDtypeStruct((t, width), BF16),
        scratch_shapes=[pltpu.VMEM((heads, seq, 2 * LANES), BF16)],
        compiler_params=_params(("parallel", "arbitrary"), 56),
        name="mla_attention",
    )(q, kv, kv, proj, proj, cc, ss, cc, ss, gain.reshape(1, width).astype(F32))


def _nat_kernel(q_ref, k_ref, v_ref, bias_ref, g_ref, o_ref, *, rows, pairs):
    r = pl.program_id(1)
    rs = jnp.clip(r - NAT_WIN_R // 2, 0, rows - NAT_WIN_R)
    koff = pl.multiple_of(rs * GRID_W, GRID_W)
    band = NAT_WIN_R * GRID_W
    lane = lax.broadcasted_iota(jnp.int32, (GRID_W, LANES), 1)
    first = lane < NAT_HEAD_DIM
    outs = []
    for j in range(pairs):
        cols = slice(j * LANES, (j + 1) * LANES)
        q2 = q_ref[:, cols] * jnp.asarray(NAT_HEAD_DIM ** -0.5, BF16)
        zero = jnp.zeros_like(q2)
        qq = jnp.concatenate([jnp.where(first, q2, zero), jnp.where(first, zero, q2)], axis=0)
        kb = k_ref[pl.ds(koff, band), cols]
        vb = v_ref[pl.ds(koff, band), cols]
        s = lax.dot_general(qq, kb, (((1,), (1,)), ((), ())), preferred_element_type=F32) + bias_ref[0, j]
        m = jnp.max(s, axis=-1, keepdims=True)
        p = jnp.exp(s - m)
        l = jnp.sum(p, axis=-1, keepdims=True)
        o = jnp.dot(p.astype(BF16), vb, preferred_element_type=F32) / l
        outs.append(jnp.where(first, o[:GRID_W], o[GRID_W:]))
    o_all = jnp.concatenate(outs, axis=1)
    o_ref[...] = _rms(o_all, g_ref[...]).astype(o_ref.dtype)


def _nat_row_start(r, rows):
    return jnp.clip(r - NAT_WIN_R // 2, 0, rows - NAT_WIN_R) - r + (NAT_WIN_R - 1)


def _nat(proj, bias_tbl, gain, *, batch, seq):
    rows = seq // GRID_W
    assert rows >= NAT_WIN_R
    t = proj.shape[0]
    pairs = bias_tbl.shape[1]
    width = pairs * LANES
    return pl.pallas_call(
        functools.partial(_nat_kernel, rows=rows, pairs=pairs),
        grid=(batch, rows),
        in_specs=[
            pl.BlockSpec((GRID_W, width), lambda b, r: (b * rows + r, 0)),
            pl.BlockSpec((seq, width), lambda b, r: (b, 1)),
            pl.BlockSpec((seq, width), lambda b, r: (b, 2)),
            pl.BlockSpec((1, pairs, 2 * GRID_W, NAT_WIN_R * GRID_W), lambda b, r: (_nat_row_start(r, rows), 0, 0, 0)),
            pl.BlockSpec((1, width), lambda b, r: (0, 0)),
        ],
        out_specs=pl.BlockSpec((GRID_W, width), lambda b, r: (b * rows + r, 0)),
        out_shape=jax.ShapeDtypeStruct((t, width), BF16),
        compiler_params=_params(("parallel", "arbitrary"), 48),
        name="nat_attention",
    )(proj, proj, proj, bias_tbl, gain.reshape(1, width).astype(F32))


def _nat_bias_tables(rpb):
    n_l, heads = rpb.shape[0], rpb.shape[1]
    qc = np.arange(GRID_W)[:, None]
    kc = np.arange(GRID_W)[None, :]
    c_start = np.clip(qc - NAT_WIN_C // 2, 0, GRID_W - NAT_WIN_C)
    valid = (kc >= c_start) & (kc < c_start + NAT_WIN_C)
    dc_idx = np.clip(kc - qc, -(NAT_WIN_C - 1), NAT_WIN_C - 1) + (NAT_WIN_C - 1)
    tbl = rpb.astype(F32)[:, :, :, dc_idx]
    tbl = jnp.where(jnp.asarray(valid), tbl, NEG_BIG)
    starts = np.arange(NAT_WIN_R)[:, None] + np.arange(NAT_WIN_R)[None, :]
    tbl = tbl[:, :, starts]
    tbl = jnp.transpose(tbl, (0, 2, 1, 4, 3, 5))
    return tbl.reshape(n_l, NAT_WIN_R, heads // 2, 2 * GRID_W, NAT_WIN_R * GRID_W)


def _top_rows(x, dst_ref, k):
    for i in range(k):
        m = jnp.max(x, axis=0, keepdims=True)
        dst_ref[i:i + 1, :] = m
        x = jnp.where(x == m, -jnp.inf, x)


def _route_kernel(ht_ref, wq_ref, keys_ref, s1_ref, s2_ref, w1_ref, w2_ref, tau_ref, a_ref, b_ref, cand_ref, top_ref):
    nk = keys_ref.shape[1]
    qt = jnp.dot(wq_ref[...], ht_ref[...], preferred_element_type=F32)
    half = qt.shape[0] // 2
    s1 = jnp.dot(keys_ref[0], qt[:half].astype(BF16), preferred_element_type=F32)
    s2 = jnp.dot(keys_ref[1], qt[half:].astype(BF16), preferred_element_type=F32)
    _top_rows(s1, a_ref, PEER_TOPK)
    _top_rows(s2, b_ref, PEER_TOPK)
    b16 = b_ref[...]
    for r in range(PEER_TOPK):
        cand_ref[r * PEER_TOPK:(r + 1) * PEER_TOPK, :] = a_ref[r:r + 1, :] + b16
    _top_rows(cand_ref[...], top_ref, PEER_TOPK)
    top = top_ref[...]
    m = top[0:1]
    z = jnp.sum(jnp.exp(top - m), axis=0, keepdims=True)
    s1_ref[0] = s1
    s2_ref[0] = s2
    w1_ref[0] = jnp.exp(s1 - a_ref[0:1, :])
    w2_ref[0] = jnp.exp(s2 - b_ref[0:1, :]) / z
    tau_ref[0] = top[PEER_TOPK - 1:PEER_TOPK]
    del nk


def _route(ht, wqt, keys, *, tm=512):
    d, t = ht.shape
    heads = PEER_HEADS
    dq2 = wqt.shape[0] // heads
    nk = keys.shape[1]
    tm = min(tm, t)
    big = jax.ShapeDtypeStruct((heads, nk, t), F32)
    big_spec = pl.BlockSpec((1, nk, tm), lambda i, h: (h, 0, i))
    return pl.pallas_call(
        _route_kernel,
        grid=(t // tm, heads),
        in_specs=[
            pl.BlockSpec((d, tm), lambda i, h: (0, i)),
            pl.BlockSpec((dq2, d), lambda i, h: (h, 0)),
            pl.BlockSpec(keys.shape, lambda i, h: (0, 0, 0)),
        ],
        out_specs=[big_spec, big_spec, big_spec, big_spec, pl.BlockSpec((1, 1, tm), lambda i, h: (h, 0, i))],
        out_shape=[big, big, big, big, jax.ShapeDtypeStruct((heads, 1, t), F32)],
        scratch_shapes=[
            pltpu.VMEM((PEER_TOPK, tm), F32),
            pltpu.VMEM((PEER_TOPK, tm), F32),
            pltpu.VMEM((PEER_TOPK * PEER_TOPK, tm), F32),
            pltpu.VMEM((PEER_TOPK, tm), F32),
        ],
        compiler_params=_params(("parallel", "arbitrary"), 40),
        name="peer_route",
    )(ht, wqt, keys)


def _gelu(x):
    return 0.5 * x * (1.0 + lax.erf(x * 0.7071067811865476))


def _experts_kernel(ht_ref, u_ref, vt_ref, s1_ref, s2_ref, w1_ref, w2_ref, tau_ref, x_ref, o_ref, acc_ref,
                    *, n1, heads):
    e = pl.program_id(1)

    @pl.when(e == 0)
    def _():
        acc_ref[...] = jnp.zeros_like(acc_ref)

    nk = s2_ref.shape[1]
    act = jnp.dot(u_ref[...], ht_ref[...], preferred_element_type=F32)
    parts = []
    for k in range(n1):
        i1 = e * n1 + k
        g = jnp.zeros((nk, act.shape[1]), F32)
        for h in range(heads):
            s1row = s1_ref[h, pl.ds(i1, 1), :]
            w1row = w1_ref[h, pl.ds(i1, 1), :]
            hit = (s2_ref[h] + s1row) >= tau_ref[h]
            g = g + jnp.where(hit, w2_ref[h] * w1row, 0.0)
        parts.append((_gelu(act[k * nk:(k + 1) * nk]) * g).astype(BF16))
    a_t = parts[0] if n1 == 1 else jnp.concatenate(parts, axis=0)
    acc_ref[...] += jnp.dot(vt_ref[...], a_t, preferred_element_type=F32)

    @pl.when(e == pl.num_programs(1) - 1)
    def _():
        o_ref[...] = x_ref[...] + acc_ref[...].T


def _experts(ht, u, vt, s1, s2, w1, w2, tau, x, *, tm=512, n1=4):
    d, t = ht.shape
    heads, nk, _ = s1.shape
    tm = min(tm, t)
    te = n1 * nk
    route_spec = pl.BlockSpec((heads, nk, tm), lambda i, e: (0, 0, i))
    return pl.pallas_call(
        functools.partial(_experts_kernel, n1=n1, heads=heads),
        grid=(t // tm, nk // n1),
        in_specs=[
            pl.BlockSpec((d, tm), lambda i, e: (0, i)),
            pl.BlockSpec((te, d), lambda i, e: (e, 0)),
            pl.BlockSpec((d, te), lambda i, e: (0, e)),
            route_spec, route_spec, route_spec, route_spec,
            pl.BlockSpec((heads, 1, tm), lambda i, e: (0, 0, i)),
            pl.BlockSpec((tm, d), lambda i, e: (i, 0)),
        ],
        out_specs=pl.BlockSpec((tm, d), lambda i, e: (i, 0)),
        out_shape=jax.ShapeDtypeStruct((t, d), F32),
        scratch_shapes=[pltpu.VMEM((d, tm), F32)],
        compiler_params=_params(("parallel", "arbitrary"), 60),
        name="peer_experts",
    )(ht, u, vt, s1, s2, w1, w2, tau, x)


def _swap_halves(w):
    half = w.shape[-1] // 2
    return jnp.concatenate([w[..., half:], w[..., :half]], axis=-1)


def _prep_w_in(w_in, q_lora, kv_lora, nat_width):
    o1 = q_lora
    o2 = o1 + kv_lora
    o3 = o2 + MLA_ROPE
    c_q, c_kv, k_pe, nat = w_in[..., :o1], w_in[..., o1:o2], w_in[..., o2:o3], w_in[..., o3:]
    assert nat.shape[-1] == 3 * nat_width
    k_sw = _swap_halves(k_pe)
    return jnp.concatenate([nat, c_q, c_kv, k_pe, k_pe, k_sw, k_sw], axis=-1).astype(BF16)


def _prep_w_uq(w_uq):
    n_l, k, _ = w_uq.shape
    w = w_uq.reshape(n_l, k, MLA_HEADS, MLA_NOPE + MLA_ROPE)
    nope = w[..., :MLA_NOPE].reshape(n_l, k, MLA_HEADS * MLA_NOPE)
    pe = w[..., MLA_NOPE:]
    return jnp.concatenate([nope, pe.reshape(n_l, k, -1), _swap_halves(pe).reshape(n_l, k, -1)], axis=-1).astype(BF16)


def _prep_w_ukv(w_ukv):
    n_l, k, n = w_ukv.shape
    w = w_ukv.reshape(n_l, k, MLA_HEADS, n // MLA_HEADS)
    return jnp.concatenate([w[..., :MLA_NOPE].reshape(n_l, k, -1), w[..., MLA_NOPE:].reshape(n_l, k, -1)],
                           axis=-1).astype(BF16)


def _rope_tables(seq):
    inv = ROPE_BASE ** (-jnp.arange(0, MLA_ROPE, 2, dtype=F32) / MLA_ROPE)
    ang = jnp.arange(seq, dtype=F32)[:, None] * inv[None, :]
    cos, sin = jnp.cos(ang), jnp.sin(ang)
    return jnp.concatenate([cos, cos, cos, cos], axis=-1), jnp.concatenate([-sin, sin, -sin, sin], axis=-1)


def kernel(x, attn_norm, w_in, mla_q_norm, mla_w_uq, mla_kv_norm, mla_w_ukv, nat_rpb, mla_out_norm, nat_out_norm,
           w_out, ffn_norm, peer_w_q, peer_sub_keys, peer_u, peer_v, final_norm):
    batch, seq, d = x.shape
    depth = w_in.shape[0]
    q_lora, kv_lora = mla_q_norm.shape[1], mla_kv_norm.shape[1]
    nat_width = nat_rpb.shape[1] * NAT_HEAD_DIM
    mla_width = mla_out_norm.shape[1]
    assert mla_w_ukv.shape[2] == MLA_HEADS * 2 * LANES and mla_width == MLA_HEADS * LANES
    assert nat_width == mla_width and q_lora % LANES == 0 and kv_lora % LANES == 0
    t = batch * seq

    w1 = _prep_w_in(w_in, q_lora, kv_lora, nat_width)
    cq_col = 3 * nat_width // q_lora
    ckv_col = (3 * nat_width + q_lora) // kv_lora
    kpe_col = (3 * nat_width + q_lora + kv_lora) // LANES
    wq = _prep_w_uq(mla_w_uq)
    wkv = _prep_w_ukv(mla_w_ukv)
    wo = w_out.astype(BF16)
    wpq_t = jnp.swapaxes(peer_w_q, 1, 2).astype(BF16)
    keys = peer_sub_keys.astype(BF16)
    u = peer_u.astype(BF16)
    vt = jnp.swapaxes(peer_v, 1, 2).astype(BF16)
    bias_tbl = _nat_bias_tables(nat_rpb)
    cc, ss = _rope_tables(seq)

    xf = x.reshape(t, d)
    for l in range(depth):
        proj = _matmul([(xf, 0, d)], w1[l], out_dtype=BF16, gain=attn_norm[l], tn=1024, name="proj_in")
        q = _matmul([(proj, cq_col, q_lora)], wq[l], out_dtype=BF16, gain=mla_q_norm[l], tn=1024, name="mla_q_up")
        kv = _matmul([(proj, ckv_col, kv_lora)], wkv[l], out_dtype=BF16, gain=mla_kv_norm[l], tn=1024, name="mla_kv_up")
        mla_o = _mla(q, kv, proj, cc, ss, mla_out_norm[l], batch=batch, seq=seq, kpe_col=kpe_col)
        nat_o = _nat(proj, bias_tbl[l], nat_out_norm[l], batch=batch, seq=seq)
        xf = _matmul([(mla_o, 0, mla_width), (nat_o, 0, nat_width)], wo[l], out_dtype=F32, res=xf, name="mix_out")
        ht = _rmsnorm(xf, ffn_norm[l], out_dtype=BF16, transpose=True)
        s1, s2, g1, g2, tau = _route(ht, wpq_t[l], keys[l])
        xf = _experts(ht, u[l], vt[l], s1, s2, g1, g2, tau, xf)
    out = _rmsnorm(xf, final_norm, out_dtype=F32)
    return out.reshape(batch, seq, d)
```

```python
import functools

import numpy as np
import jax
import jax.numpy as jnp
from jax import lax
from jax.experimental import pallas as pl
from jax.experimental.pallas import tpu as pltpu

F32 = jnp.float32
BF16 = jnp.bfloat16

RMS_EPS = 1e-6
GRID_W = 64
MLA_HEADS = 8
MLA_NOPE = 128
MLA_ROPE = 64
ROPE_BASE = 10000.0
NAT_HEAD_DIM = 64
NAT_WIN_R = 8
NAT_WIN_C = 16
PEER_HEADS = 8
PEER_TOPK = 16
LANES = 128
NEG_BIG = -1e30
MIB = 1024 * 1024


def _params(sem, vmem_mib):
    return pltpu.CompilerParams(dimension_semantics=sem, vmem_limit_bytes=vmem_mib * MIB)


def _rms(x, g):
    ms = jnp.mean(x * x, axis=-1, keepdims=True)
    return x * lax.rsqrt(ms + RMS_EPS) * g


def _rmsnorm_kernel(x_ref, g_ref, o_ref, *, transpose):
    y = _rms(x_ref[...].astype(F32), g_ref[...])
    if transpose:
        y = y.T
    o_ref[...] = y.astype(o_ref.dtype)


def _rmsnorm(x, g, *, out_dtype, transpose=False, tm=512):
    t, d = x.shape
    tm = min(tm, t)
    if transpose:
        out_shape = jax.ShapeDtypeStruct((d, t), out_dtype)
        out_spec = pl.BlockSpec((d, tm), lambda i: (0, i))
    else:
        out_shape = jax.ShapeDtypeStruct((t, d), out_dtype)
        out_spec = pl.BlockSpec((tm, d), lambda i: (i, 0))
    return pl.pallas_call(
        functools.partial(_rmsnorm_kernel, transpose=transpose),
        grid=(t // tm,),
        in_specs=[pl.BlockSpec((tm, d), lambda i: (i, 0)), pl.BlockSpec((1, d), lambda i: (0, 0))],
        out_specs=out_spec,
        out_shape=out_shape,
        compiler_params=_params(("parallel",), 40),
        name="rmsnorm_t" if transpose else "rmsnorm",
    )(x, g.reshape(1, d).astype(F32))


def _mm_kernel(*refs, n_a, norm, has_res):
    a_refs = refs[:n_a]
    pos = n_a
    g_ref = refs[pos] if norm else None
    pos += int(norm)
    w_ref = refs[pos]
    pos += 1
    res_ref = refs[pos] if has_res else None
    pos += int(has_res)
    o_ref, an_ref = refs[pos], refs[pos + 1]

    @pl.when(pl.program_id(1) == 0)
    def _():
        parts = [r[...] for r in a_refs]
        a = parts[0] if n_a == 1 else jnp.concatenate(parts, axis=1)
        if norm:
            a = _rms(a.astype(F32), g_ref[...])
        an_ref[...] = a.astype(BF16)

    acc = jnp.dot(an_ref[...], w_ref[...], preferred_element_type=F32)
    if has_res:
        acc = acc + res_ref[...]
    o_ref[...] = acc.astype(o_ref.dtype)


def _matmul(a_list, w, *, out_dtype, gain=None, res=None, tm=512, tn=512, name="matmul"):
    t = a_list[0][0].shape[0]
    k, n = w.shape
    assert k == sum(width for _, _, width in a_list)
    tm, tn = min(tm, t), min(tn, n)
    in_specs, args = [], []
    for arr, cb, width in a_list:
        in_specs.append(pl.BlockSpec((tm, width), lambda i, j, cb=cb: (i, cb)))
        args.append(arr)
    if gain is not None:
        in_specs.append(pl.BlockSpec((1, k), lambda i, j: (0, 0)))
        args.append(gain.reshape(1, k).astype(F32))
    in_specs.append(pl.BlockSpec((k, tn), lambda i, j: (0, j)))
    args.append(w)
    if res is not None:
        in_specs.append(pl.BlockSpec((tm, tn), lambda i, j: (i, j)))
        args.append(res)
    return pl.pallas_call(
        functools.partial(_mm_kernel, n_a=len(a_list), norm=gain is not None, has_res=res is not None),
        grid=(t // tm, n // tn),
        in_specs=in_specs,
        out_specs=pl.BlockSpec((tm, tn), lambda i, j: (i, j)),
        out_shape=jax.ShapeDtypeStruct((t, n), out_dtype),
        scratch_shapes=[pltpu.VMEM((tm, k), BF16)],
        compiler_params=_params(("parallel", "arbitrary"), 48),
        name=name,
    )(*args)


def _mla_kernel(q_ref, k_ref, v_ref, ka_ref, kb_ref, cck_ref, ssk_ref, ccq_ref, ssq_ref, g_ref, o_ref, kcat_ref,
                *, heads, scale):
    @pl.when(pl.program_id(1) == 0)
    def _():
        rk = ka_ref[...].astype(F32) * cck_ref[...] + kb_ref[...].astype(F32) * ssk_ref[...]
        rk = rk.astype(BF16)
        for h in range(heads):
            kcat_ref[h, :, 0:LANES] = k_ref[:, h * LANES:(h + 1) * LANES]
            kcat_ref[h, :, LANES:2 * LANES] = rk

    tq = q_ref.shape[0]
    pe0 = heads * MLA_NOPE
    sw0 = pe0 + (heads // 2) * LANES
    ccq, ssq = ccq_ref[...], ssq_ref[...]
    lane = lax.broadcasted_iota(jnp.int32, (tq, LANES), 1)
    outs = []
    for h in range(heads):
        p_lo = pe0 + (h // 2) * LANES
        s_lo = sw0 + (h // 2) * LANES
        rq = q_ref[:, p_lo:p_lo + LANES].astype(F32) * ccq + q_ref[:, s_lo:s_lo + LANES].astype(F32) * ssq
        keep = (lane < MLA_ROPE) if h % 2 == 0 else (lane >= MLA_ROPE)
        rq = jnp.where(keep, rq, 0.0).astype(BF16)
        qcat = jnp.concatenate([q_ref[:, h * LANES:(h + 1) * LANES], rq], axis=1)
        s = lax.dot_general(qcat, kcat_ref[h], (((1,), (1,)), ((), ())), preferred_element_type=F32) * scale
        m = jnp.max(s, axis=-1, keepdims=True)
        p = jnp.exp(s - m)
        l = jnp.sum(p, axis=-1, keepdims=True)
        o = jnp.dot(p.astype(BF16), v_ref[:, h * LANES:(h + 1) * LANES], preferred_element_type=F32)
        outs.append(o / l)
    o_all = jnp.concatenate(outs, axis=1)
    o_ref[...] = _rms(o_all, g_ref[...]).astype(o_ref.dtype)


def _mla(q, kv, proj, cc, ss, gain, *, batch, seq, kpe_col, tq=256):
    heads = MLA_HEADS
    t = q.shape[0]
    tq = min(tq, seq)
    nq = seq // tq
    width = heads * LANES
    return pl.pallas_call(
        functools.partial(_mla_kernel, heads=heads, scale=float((MLA_NOPE + MLA_ROPE) ** -0.5)),
        grid=(batch, nq),
        in_specs=[
            pl.BlockSpec((tq, q.shape[1]), lambda b, i: (b * nq + i, 0)),
            pl.BlockSpec((seq, width), lambda b, i: (b, 0)),
            pl.BlockSpec((seq, width), lambda b, i: (b, 1)),
            pl.BlockSpec((seq, LANES), lambda b, i: (b, kpe_col)),
            pl.BlockSpec((seq, LANES), lambda b, i: (b, kpe_col + 1)),
            pl.BlockSpec((seq, LANES), lambda b, i: (0, 0)),
            pl.BlockSpec((seq, LANES), lambda b, i: (0, 0)),
            pl.BlockSpec((tq, LANES), lambda b, i: (i, 0)),
            pl.BlockSpec((tq, LANES), lambda b, i: (i, 0)),
            pl.BlockSpec((1, width), lambda b, i: (0, 0)),
        ],
        out_specs=pl.BlockSpec((tq, width), lambda b, i: (b * nq + i, 0)),
        out_shape=jax.ShapeDtypeStruct((t, width), BF16),
        scratch_shapes=[pltpu.VMEM((heads, seq, 2 * LANES), BF16)],
        compiler_params=_params(("parallel", "arbitrary"), 56),
        name="mla_attention",
    )(q, kv, kv, proj, proj, cc, ss, cc, ss, gain.reshape(1, width).astype(F32))


def _nat_kernel(q_ref, k_ref, v_ref, bias_ref, g_ref, o_ref, *, rows, pairs):
    r = pl.program_id(1)
    rs = jnp.clip(r - NAT_WIN_R // 2, 0, rows - NAT_WIN_R)
    koff = pl.multiple_of(rs * GRID_W, GRID_W)
    band = NAT_WIN_R * GRID_W
    lane = lax.broadcasted_iota(jnp.int32, (GRID_W, LANES), 1)
    first = lane < NAT_HEAD_DIM
    outs = []
    for j in range(pairs):
        cols = slice(j * LANES, (j + 1) * LANES)
        q2 = q_ref[:, cols] * jnp.asarray(NAT_HEAD_DIM ** -0.5, BF16)
        zero = jnp.zeros_like(q2)
        qq = jnp.concatenate([jnp.where(first, q2, zero), jnp.where(first, zero, q2)], axis=0)
        kb = k_ref[pl.ds(koff, band), cols]
        vb = v_ref[pl.ds(koff, band), cols]
        s = lax.dot_general(qq, kb, (((1,), (1,)), ((), ())), preferred_element_type=F32) + bias_ref[0, j]
        m = jnp.max(s, axis=-1, keepdims=True)
        p = jnp.exp(s - m)
        l = jnp.sum(p, axis=-1, keepdims=True)
        o = jnp.dot(p.astype(BF16), vb, preferred_element_type=F32) / l
        outs.append(jnp.where(first, o[:GRID_W], o[GRID_W:]))
    o_all = jnp.concatenate(outs, axis=1)
    o_ref[...] = _rms(o_all, g_ref[...]).astype(o_ref.dtype)


def _nat_row_start(r, rows):
    return jnp.clip(r - NAT_WIN_R // 2, 0, rows - NAT_WIN_R) - r + (NAT_WIN_R - 1)


def _nat(proj, bias_tbl, gain, *, batch, seq):
    rows = seq // GRID_W
    assert rows >= NAT_WIN_R
    t = proj.shape[0]
    pairs = bias_tbl.shape[1]
    width = pairs * LANES
    return pl.pallas_call(
        functools.partial(_nat_kernel, rows=rows, pairs=pairs),
        grid=(batch, rows),
        in_specs=[
            pl.BlockSpec((GRID_W, width), lambda b, r: (b * rows + r, 0)),
            pl.BlockSpec((seq, width), lambda b, r: (b, 1)),
            pl.BlockSpec((seq, width), lambda b, r: (b, 2)),
            pl.BlockSpec((1, pairs, 2 * GRID_W, NAT_WIN_R * GRID_W), lambda b, r: (_nat_row_start(r, rows), 0, 0, 0)),
            pl.BlockSpec((1, width), lambda b, r: (0, 0)),
        ],
        out_specs=pl.BlockSpec((GRID_W, width), lambda b, r: (b * rows + r, 0)),
        out_shape=jax.ShapeDtypeStruct((t, width), BF16),
        compiler_params=_params(("parallel", "arbitrary"), 48),
        name="nat_attention",
    )(proj, proj, proj, bias_tbl, gain.reshape(1, width).astype(F32))


def _nat_bias_tables(rpb):
    n_l, heads = rpb.shape[0], rpb.shape[1]
    qc = np.arange(GRID_W)[:, None]
    kc = np.arange(GRID_W)[None, :]
    c_start = np.clip(qc - NAT_WIN_C // 2, 0, GRID_W - NAT_WIN_C)
    valid = (kc >= c_start) & (kc < c_start + NAT_WIN_C)
    dc_idx = np.clip(kc - qc, -(NAT_WIN_C - 1), NAT_WIN_C - 1) + (NAT_WIN_C - 1)
    tbl = rpb.astype(F32)[:, :, :, dc_idx]
    tbl = jnp.where(jnp.asarray(valid), tbl, NEG_BIG)
    starts = np.arange(NAT_WIN_R)[:, None] + np.arange(NAT_WIN_R)[None, :]
    tbl = tbl[:, :, starts]
    tbl = jnp.transpose(tbl, (0, 2, 1, 4, 3, 5))
    return tbl.reshape(n_l, NAT_WIN_R, heads // 2, 2 * GRID_W, NAT_WIN_R * GRID_W)


SUB16 = 16


def _top_rows_ranked(x, dst_ref, k):
    rank = jnp.full(x.shape, float(k), F32)
    for i in range(k):
        m = jnp.max(x, axis=0, keepdims=True)
        dst_ref[i:i + 1, :] = m
        hit = x == m
        rank = jnp.where(hit, float(i), rank)
        x = jnp.where(hit, -jnp.inf, x)
    return rank


def _top2_rows(x1, x2, dst1_ref, dst2_ref, k):
    rank2 = jnp.full(x2.shape, float(k), F32)
    for i in range(k):
        m1 = jnp.max(x1, axis=0, keepdims=True)
        m2 = jnp.max(x2, axis=0, keepdims=True)
        dst1_ref[i:i + 1, :] = m1
        dst2_ref[i:i + 1, :] = m2
        hit2 = x2 == m2
        x1 = jnp.where(x1 == m1, -jnp.inf, x1)
        rank2 = jnp.where(hit2, float(i), rank2)
        x2 = jnp.where(hit2, -jnp.inf, x2)
    return rank2


def _dup_bf16_bits(x):
    bits = pltpu.bitcast(x.astype(BF16).astype(F32), jnp.uint32)
    return bits | (bits >> 16)


def _route_kernel(ht_ref, wq_ref, keys_ref, cnt_ref, c1_ref, r2_ref, e2_ref, a_ref, b_ref, cand_ref, top_ref):
    nk = keys_ref.shape[1]
    tm = ht_ref.shape[1]
    qt = jnp.dot(wq_ref[...], ht_ref[...], preferred_element_type=F32)
    half = qt.shape[0] // 2
    s1_all = jnp.dot(keys_ref[0], qt[:half].astype(BF16), preferred_element_type=F32)
    s2_all = jnp.dot(keys_ref[1], qt[half:].astype(BF16), preferred_element_type=F32)
    for c in range(tm // LANES):
        lanes = slice(c * LANES, (c + 1) * LANES)
        s1, s2 = s1_all[:, lanes], s2_all[:, lanes]
        rank2 = _top2_rows(s1, s2, a_ref, b_ref, PEER_TOPK)
        offs, off = [], 0
        for r in range(PEER_TOPK):
            n_r = PEER_TOPK // (r + 1)
            cand_ref[off:off + n_r, :] = a_ref[r:r + 1, :] + b_ref[0:n_r, :]
            offs.append((off, n_r))
            off += n_r
        if off < cand_ref.shape[0]:
            cand_ref[off:, :] = jnp.full((cand_ref.shape[0] - off, LANES), -jnp.inf, F32)
        taken = _top_rows_ranked(cand_ref[...], top_ref, PEER_TOPK) < float(PEER_TOPK)
        taken = taken.astype(F32)
        top = top_ref[...]
        z = jnp.sum(jnp.exp(top - top[0:1]), axis=0, keepdims=True)
        cnt = jnp.zeros((nk, LANES), F32)
        for r, (off, n_r) in enumerate(offs):
            cnt_r = jnp.sum(taken[off:off + n_r], axis=0, keepdims=True)
            cnt = jnp.where(s1 == a_ref[r:r + 1, :], cnt_r, cnt)
        c1 = jnp.exp(s1 - a_ref[0:1, :]) / z
        e2 = jnp.exp(s2 - b_ref[0:1, :])
        cnt_ref[0, :, lanes] = _dup_bf16_bits(cnt)
        c1_ref[0, :, lanes] = _dup_bf16_bits(c1)
        r2_ref[0, :, lanes] = rank2.astype(BF16)
        e2_ref[0, :, lanes] = e2.astype(BF16)


def _route(ht, wqt, keys, *, tm=512):
    d, t = ht.shape
    heads = PEER_HEADS
    dq2 = wqt.shape[0] // heads
    nk = keys.shape[1]
    tm = min(tm, t)
    n_cand = -(-sum(PEER_TOPK // (r + 1) for r in range(PEER_TOPK)) // 8) * 8
    row_shape = jax.ShapeDtypeStruct((heads, nk, t), jnp.uint32)
    row_spec = pl.BlockSpec((1, nk, tm), lambda i, h: (h, 0, i))
    tile_shape = jax.ShapeDtypeStruct((heads, nk, t), BF16)
    tile_spec = pl.BlockSpec((1, nk, tm), lambda i, h: (h, 0, i))
    return pl.pallas_call(
        _route_kernel,
        grid=(t // tm, heads),
        in_specs=[
            pl.BlockSpec((d, tm), lambda i, h: (0, i)),
            pl.BlockSpec((dq2, d), lambda i, h: (h, 0)),
            pl.BlockSpec(keys.shape, lambda i, h: (0, 0, 0)),
        ],
        out_specs=[row_spec, row_spec, tile_spec, tile_spec],
        out_shape=[row_shape, row_shape, tile_shape, tile_shape],
        scratch_shapes=[
            pltpu.VMEM((PEER_TOPK, LANES), F32),
            pltpu.VMEM((PEER_TOPK, LANES), F32),
            pltpu.VMEM((n_cand, LANES), F32),
            pltpu.VMEM((PEER_TOPK, LANES), F32),
        ],
        compiler_params=_params(("parallel", "arbitrary"), 40),
        name="peer_route",
    )(ht, wqt, keys)


def _gelu(x):
    return 0.5 * x * (1.0 + lax.erf(x * 0.7071067811865476))


GATE_LANES = 256
GATE_KGROUP = 2


def _row_as_bf16(row):
    return pltpu.bitcast(jnp.broadcast_to(row, (8, row.shape[1])), BF16)


def _gates_kernel(cnt_ref, c1_ref, r2_ref, e2_ref, g_ref, *, n1, heads):
    nk = r2_ref.shape[1]
    tm = g_ref.shape[1]
    n_sub = nk // SUB16

    def lane_chunk(c, carry):
        lanes = slice(c * GATE_LANES, (c + 1) * GATE_LANES)
        for k0 in range(0, n1, GATE_KGROUP):
            ks = range(k0, min(k0 + GATE_KGROUP, n1))
            g = {(k, s): jnp.zeros((SUB16, GATE_LANES), BF16) for k in ks for s in range(n_sub)}
            for h in range(heads):
                cnt = {k: _row_as_bf16(cnt_ref[h, 0, k:k + 1, lanes]) for k in ks}
                c1 = {k: _row_as_bf16(c1_ref[h, 0, k:k + 1, lanes]) for k in ks}
                for s in range(n_sub):
                    rows = slice(s * SUB16, (s + 1) * SUB16)
                    r2 = r2_ref[h, rows, lanes]
                    e2 = e2_ref[h, rows, lanes]
                    for k in ks:
                        g[k, s] = g[k, s] + jnp.where(r2 < cnt[k], e2, jnp.zeros_like(e2)) * c1[k]
            for k in ks:
                for s in range(n_sub):
                    g_ref[k * nk + s * SUB16:k * nk + (s + 1) * SUB16, lanes] = g[k, s]
        return carry

    for c in range(tm // GATE_LANES):
        lane_chunk(c, 0)


def _gates(cnt, c1, r2, e2, *, tm=512, n1=4):
    heads, nk, t = cnt.shape
    tm = min(tm, t)
    nb = nk // n1
    rows = lambda a: a.reshape(heads, nb, n1, t)
    row_spec = pl.BlockSpec((heads, 1, n1, tm), lambda i, e: (0, e, 0, i))
    tile_spec = pl.BlockSpec((heads, nk, tm), lambda i, e: (0, 0, i))
    return pl.pallas_call(
        functools.partial(_gates_kernel, n1=n1, heads=heads),
        grid=(t // tm, nb),
        in_specs=[row_spec, row_spec, tile_spec, tile_spec],
        out_specs=pl.BlockSpec((n1 * nk, tm), lambda i, e: (e, i)),
        out_shape=jax.ShapeDtypeStruct((nk * nk, t), BF16),
        compiler_params=_params(("parallel", "arbitrary"), 40),
        name="peer_gates",
    )(rows(cnt), rows(c1), r2, e2)


def _experts_kernel(ht_ref, u_ref, vt_ref, g_ref, x_ref, o_ref, acc_ref):
    e = pl.program_id(1)

    @pl.when(e == 0)
    def _():
        acc_ref[...] = jnp.zeros_like(acc_ref)

    act = jnp.dot(u_ref[...], ht_ref[...], preferred_element_type=F32)
    a_t = _gelu(act).astype(BF16) * g_ref[...]
    acc_ref[...] += jnp.dot(vt_ref[...], a_t, preferred_element_type=F32)

    @pl.when(e == pl.num_programs(1) - 1)
    def _():
        o_ref[...] = x_ref[...] + acc_ref[...].T


def _experts(ht, u, vt, gates, x, *, tm=512, te=512):
    d, t = ht.shape
    n_e = u.shape[0]
    tm, te = min(tm, t), min(te, n_e)
    return pl.pallas_call(
        _experts_kernel,
        grid=(t // tm, n_e // te),
        in_specs=[
            pl.BlockSpec((d, tm), lambda i, e: (0, i)),
            pl.BlockSpec((te, d), lambda i, e: (e, 0)),
            pl.BlockSpec((d, te), lambda i, e: (0, e)),
            pl.BlockSpec((te, tm), lambda i, e: (e, i)),
            pl.BlockSpec((tm, d), lambda i, e: (i, 0)),
        ],
        out_specs=pl.BlockSpec((tm, d), lambda i, e: (i, 0)),
        out_shape=jax.ShapeDtypeStruct((t, d), F32),
        scratch_shapes=[pltpu.VMEM((d, tm), F32)],
        compiler_params=_params(("parallel", "arbitrary"), 56),
        name="peer_experts",
    )(ht, u, vt, gates, x)


def _swap_halves(w):
    half = w.shape[-1] // 2
    return jnp.concatenate([w[..., half:], w[..., :half]], axis=-1)


def _prep_w_in(w_in, q_lora, kv_lora, nat_width):
    o1 = q_lora
    o2 = o1 + kv_lora
    o3 = o2 + MLA_ROPE
    c_q, c_kv, k_pe, nat = w_in[..., :o1], w_in[..., o1:o2], w_in[..., o2:o3], w_in[..., o3:]
    assert nat.shape[-1] == 3 * nat_width
    k_sw = _swap_halves(k_pe)
    return jnp.concatenate([nat, c_q, c_kv, k_pe, k_pe, k_sw, k_sw], axis=-1).astype(BF16)


def _prep_w_uq(w_uq):
    n_l, k, _ = w_uq.shape
    w = w_uq.reshape(n_l, k, MLA_HEADS, MLA_NOPE + MLA_ROPE)
    nope = w[..., :MLA_NOPE].reshape(n_l, k, MLA_HEADS * MLA_NOPE)
    pe = w[..., MLA_NOPE:]
    return jnp.concatenate([nope, pe.reshape(n_l, k, -1), _swap_halves(pe).reshape(n_l, k, -1)], axis=-1).astype(BF16)


def _prep_w_ukv(w_ukv):
    n_l, k, n = w_ukv.shape
    w = w_ukv.reshape(n_l, k, MLA_HEADS, n // MLA_HEADS)
    return jnp.concatenate([w[..., :MLA_NOPE].reshape(n_l, k, -1), w[..., MLA_NOPE:].reshape(n_l, k, -1)],
                           axis=-1).astype(BF16)


def _rope_tables(seq):
    inv = ROPE_BASE ** (-jnp.arange(0, MLA_ROPE, 2, dtype=F32) / MLA_ROPE)
    ang = jnp.arange(seq, dtype=F32)[:, None] * inv[None, :]
    cos, sin = jnp.cos(ang), jnp.sin(ang)
    return jnp.concatenate([cos, cos, cos, cos], axis=-1), jnp.concatenate([-sin, sin, -sin, sin], axis=-1)


def kernel(x, attn_norm, w_in, mla_q_norm, mla_w_uq, mla_kv_norm, mla_w_ukv, nat_rpb, mla_out_norm, nat_out_norm,
           w_out, ffn_norm, peer_w_q, peer_sub_keys, peer_u, peer_v, final_norm):
    batch, seq, d = x.shape
    depth = w_in.shape[0]
    q_lora, kv_lora = mla_q_norm.shape[1], mla_kv_norm.shape[1]
    nat_width = nat_rpb.shape[1] * NAT_HEAD_DIM
    mla_width = mla_out_norm.shape[1]
    assert mla_w_ukv.shape[2] == MLA_HEADS * 2 * LANES and mla_width == MLA_HEADS * LANES
    assert nat_width == mla_width and q_lora % LANES == 0 and kv_lora % LANES == 0
    t = batch * seq

    w1 = _prep_w_in(w_in, q_lora, kv_lora, nat_width)
    cq_col = 3 * nat_width // q_lora
    ckv_col = (3 * nat_width + q_lora) // kv_lora
    kpe_col = (3 * nat_width + q_lora + kv_lora) // LANES
    wq = _prep_w_uq(mla_w_uq)
    wkv = _prep_w_ukv(mla_w_ukv)
    wo = w_out.astype(BF16)
    wpq_t = jnp.swapaxes(peer_w_q, 1, 2).astype(BF16)
    keys = peer_sub_keys.astype(BF16)
    u = peer_u.astype(BF16)
    vt = jnp.swapaxes(peer_v, 1, 2).astype(BF16)
    bias_tbl = _nat_bias_tables(nat_rpb)
    cc, ss = _rope_tables(seq)

    xf = x.reshape(t, d)
    for l in range(depth):
        proj = _matmul([(xf, 0, d)], w1[l], out_dtype=BF16, gain=attn_norm[l], tn=1024, name="proj_in")
        q = _matmul([(proj, cq_col, q_lora)], wq[l], out_dtype=BF16, gain=mla_q_norm[l], tn=1024, name="mla_q_up")
        kv = _matmul([(proj, ckv_col, kv_lora)], wkv[l], out_dtype=BF16, gain=mla_kv_norm[l], tn=1024, name="mla_kv_up")
        mla_o = _mla(q, kv, proj, cc, ss, mla_out_norm[l], batch=batch, seq=seq, kpe_col=kpe_col)
        nat_o = _nat(proj, bias_tbl[l], nat_out_norm[l], batch=batch, seq=seq)
        xf = _matmul([(mla_o, 0, mla_width), (nat_o, 0, nat_width)], wo[l], out_dtype=F32, res=xf, name="mix_out")
        ht = _rmsnorm(xf, ffn_norm[l], out_dtype=BF16, transpose=True)
        cnt, c1, r2, e2 = _route(ht, wpq_t[l], keys[l])
        xf = _experts(ht, u[l], vt[l], _gates(cnt, c1, r2, e2), xf)
    out = _rmsnorm(xf, final_norm, out_dtype=F32)
    return out.reshape(batch, seq, d)
```

```python
import functools

import numpy as np
import jax
import jax.numpy as jnp
from jax import lax
from jax.experimental import pallas as pl
from jax.experimental.pallas import tpu as pltpu

F32 = jnp.float32
BF16 = jnp.bfloat16

RMS_EPS = 1e-6
GRID_W = 64
MLA_HEADS = 8
MLA_NOPE = 128
MLA_ROPE = 64
ROPE_BASE = 10000.0
NAT_HEAD_DIM = 64
NAT_WIN_R = 8
NAT_WIN_C = 16
PEER_HEADS = 8
PEER_TOPK = 16
LANES = 128
SUB16 = 16
NEG_BIG = -1e30
MIB = 1024 * 1024


def _params(sem, vmem_mib):
    return pltpu.CompilerParams(dimension_semantics=sem, vmem_limit_bytes=vmem_mib * MIB)


def _rms(x, g):
    ms = jnp.mean(x * x, axis=-1, keepdims=True)
    return x * lax.rsqrt(ms + RMS_EPS) * g


def _rmsnorm_kernel(x_ref, g_ref, o_ref, *, transpose):
    y = _rms(x_ref[...].astype(F32), g_ref[...])
    if transpose:
        y = y.T
    o_ref[...] = y.astype(o_ref.dtype)


def _rmsnorm(x, g, *, out_dtype, transpose=False, tm=512):
    t, d = x.shape
    tm = min(tm, t)
    if transpose:
        out_shape = jax.ShapeDtypeStruct((d, t), out_dtype)
        out_spec = pl.BlockSpec((d, tm), lambda i: (0, i))
    else:
        out_shape = jax.ShapeDtypeStruct((t, d), out_dtype)
        out_spec = pl.BlockSpec((tm, d), lambda i: (i, 0))
    return pl.pallas_call(
        functools.partial(_rmsnorm_kernel, transpose=transpose),
        grid=(t // tm,),
        in_specs=[pl.BlockSpec((tm, d), lambda i: (i, 0)), pl.BlockSpec((1, d), lambda i: (0, 0))],
        out_specs=out_spec,
        out_shape=out_shape,
        compiler_params=_params(("parallel",), 40),
        name="rmsnorm_t" if transpose else "rmsnorm",
    )(x, g.reshape(1, d).astype(F32))


def _mm_kernel(*refs, n_a, norm, has_res):
    a_refs = refs[:n_a]
    pos = n_a
    g_ref = refs[pos] if norm else None
    pos += int(norm)
    w_ref = refs[pos]
    pos += 1
    res_ref = refs[pos] if has_res else None
    pos += int(has_res)
    o_ref, an_ref = refs[pos], refs[pos + 1]

    @pl.when(pl.program_id(1) == 0)
    def _():
        parts = [r[...] for r in a_refs]
        a = parts[0] if n_a == 1 else jnp.concatenate(parts, axis=1)
        if norm:
            a = _rms(a.astype(F32), g_ref[...])
        an_ref[...] = a.astype(BF16)

    acc = jnp.dot(an_ref[...], w_ref[...], preferred_element_type=F32)
    if has_res:
        acc = acc + res_ref[...]
    o_ref[...] = acc.astype(o_ref.dtype)


def _matmul(a_list, w, *, out_dtype, gain=None, res=None, tm=512, tn=512, name="matmul"):
    t = a_list[0][0].shape[0]
    k, n = w.shape
    assert k == sum(width for _, _, width in a_list)
    tm, tn = min(tm, t), min(tn, n)
    in_specs, args = [], []
    for arr, cb, width in a_list:
        in_specs.append(pl.BlockSpec((tm, width), lambda i, j, cb=cb: (i, cb)))
        args.append(arr)
    if gain is not None:
        in_specs.append(pl.BlockSpec((1, k), lambda i, j: (0, 0)))
        args.append(gain.reshape(1, k).astype(F32))
    in_specs.append(pl.BlockSpec((k, tn), lambda i, j: (0, j)))
    args.append(w)
    if res is not None:
        in_specs.append(pl.BlockSpec((tm, tn), lambda i, j: (i, j)))
        args.append(res)
    return pl.pallas_call(
        functools.partial(_mm_kernel, n_a=len(a_list), norm=gain is not None, has_res=res is not None),
        grid=(t // tm, n // tn),
        in_specs=in_specs,
        out_specs=pl.BlockSpec((tm, tn), lambda i, j: (i, j)),
        out_shape=jax.ShapeDtypeStruct((t, n), out_dtype),
        scratch_shapes=[pltpu.VMEM((tm, k), BF16)],
        compiler_params=_params(("parallel", "arbitrary"), 48),
        name=name,
    )(*args)


def _mla_kernel(q_ref, k_ref, v_ref, ka_ref, kb_ref, cck_ref, ssk_ref, ccq_ref, ssq_ref, g_ref, o_ref, kcat_ref,
                s_ref, p_ref, *, heads, scale):
    @pl.when(pl.program_id(1) == 0)
    def _():
        rk = ka_ref[...].astype(F32) * cck_ref[...] + kb_ref[...].astype(F32) * ssk_ref[...]
        rk = rk.astype(BF16)
        for h in range(heads):
            kcat_ref[h, :, 0:LANES] = k_ref[:, h * LANES:(h + 1) * LANES]
            kcat_ref[h, :, LANES:2 * LANES] = rk

    tq = q_ref.shape[0]
    pe0 = heads * MLA_NOPE
    sw0 = pe0 + (heads // 2) * LANES
    ccq, ssq = ccq_ref[...], ssq_ref[...]
    lane = lax.broadcasted_iota(jnp.int32, (tq, LANES), 1)

    def scores(h):
        p_lo = pe0 + (h // 2) * LANES
        s_lo = sw0 + (h // 2) * LANES
        rq = q_ref[:, p_lo:p_lo + LANES].astype(F32) * ccq + q_ref[:, s_lo:s_lo + LANES].astype(F32) * ssq
        keep = (lane < MLA_ROPE) if h % 2 == 0 else (lane >= MLA_ROPE)
        rq = jnp.where(keep, rq, 0.0).astype(BF16)
        qcat = jnp.concatenate([q_ref[:, h * LANES:(h + 1) * LANES], rq], axis=1)
        s_ref[h % 2] = lax.dot_general(qcat, kcat_ref[h], (((1,), (1,)), ((), ())), preferred_element_type=F32)

    def softmax(h):
        c = scale * 1.4426950408889634
        inv = []
        for r in range(tq // SUB16):
            rows = slice(r * SUB16, (r + 1) * SUB16)
            s = s_ref[h % 2, rows, :]
            m = jnp.max(s, axis=-1, keepdims=True)
            p = jnp.exp2((s - m) * c)
            inv.append(1.0 / jnp.sum(p, axis=-1, keepdims=True))
            p_ref[h % 2, rows, :] = p.astype(BF16)
        return jnp.concatenate(inv, axis=0)

    def values(h, inv_l):
        return jnp.dot(p_ref[h % 2], v_ref[:, h * LANES:(h + 1) * LANES], preferred_element_type=F32) * inv_l

    outs, inv_l = [], {}
    scores(0)
    for h in range(heads):
        if h + 1 < heads:
            scores(h + 1)
        inv_l[h] = softmax(h)
        if h >= 1:
            outs.append(values(h - 1, inv_l[h - 1]))
    outs.append(values(heads - 1, inv_l[heads - 1]))
    o_all = jnp.concatenate(outs, axis=1)
    o_ref[...] = _rms(o_all, g_ref[...]).astype(o_ref.dtype)


def _mla(q, kv, proj, cc, ss, gain, *, batch, seq, kpe_col, tq=256):
    heads = MLA_HEADS
    t = q.shape[0]
    tq = min(tq, seq)
    nq = seq // tq
    width = heads * LANES
    return pl.pallas_call(
        functools.partial(_mla_kernel, heads=heads, scale=float((MLA_NOPE + MLA_ROPE) ** -0.5)),
        grid=(batch, nq),
        in_specs=[
            pl.BlockSpec((tq, q.shape[1]), lambda b, i: (b * nq + i, 0)),
            pl.BlockSpec((seq, width), lambda b, i: (b, 0)),
            pl.BlockSpec((seq, width), lambda b, i: (b, 1)),
            pl.BlockSpec((seq, LANES), lambda b, i: (b, kpe_col)),
            pl.BlockSpec((seq, LANES), lambda b, i: (b, kpe_col + 1)),
            pl.BlockSpec((seq, LANES), lambda b, i: (0, 0)),
            pl.BlockSpec((seq, LANES), lambda b, i: (0, 0)),
            pl.BlockSpec((tq, LANES), lambda b, i: (i, 0)),
            pl.BlockSpec((tq, LANES), lambda b, i: (i, 0)),
            pl.BlockSpec((1, width), lambda b, i: (0, 0)),
        ],
        out_specs=pl.BlockSpec((tq, width), lambda b, i: (b * nq + i, 0)),
        out_shape=jax.ShapeDtypeStruct((t, width), BF16),
        scratch_shapes=[
            pltpu.VMEM((heads, seq, 2 * LANES), BF16),
            pltpu.VMEM((2, tq, seq), F32),
            pltpu.VMEM((2, tq, seq), BF16),
        ],
        compiler_params=_params(("parallel", "arbitrary"), 56),
        name="mla_attention",
    )(q, kv, kv, proj, proj, cc, ss, cc, ss, gain.reshape(1, width).astype(F32))


def _nat_kernel(q_ref, k_ref, v_ref, bias_ref, g_ref, o_ref, s_ref, p_ref, *, rows, pairs):
    r = pl.program_id(1)
    rs = jnp.clip(r - NAT_WIN_R // 2, 0, rows - NAT_WIN_R)
    koff = pl.multiple_of(rs * GRID_W, GRID_W)
    band = NAT_WIN_R * GRID_W
    lane = lax.broadcasted_iota(jnp.int32, (GRID_W, LANES), 1)
    first = lane < NAT_HEAD_DIM
    for j in range(pairs):
        cols = slice(j * LANES, (j + 1) * LANES)
        q2 = q_ref[:, cols] * jnp.asarray(NAT_HEAD_DIM ** -0.5, BF16)
        zero = jnp.zeros_like(q2)
        qq = jnp.concatenate([jnp.where(first, q2, zero), jnp.where(first, zero, q2)], axis=0)
        kb = k_ref[pl.ds(koff, band), cols]
        s_ref[j] = lax.dot_general(qq, kb, (((1,), (1,)), ((), ())), preferred_element_type=F32)
    inv_l = []
    for j in range(pairs):
        s = s_ref[j] + bias_ref[0, j]
        m = jnp.max(s, axis=-1, keepdims=True)
        p = jnp.exp(s - m)
        inv_l.append(1.0 / jnp.sum(p, axis=-1, keepdims=True))
        p_ref[j] = p.astype(BF16)
    outs = []
    for j in range(pairs):
        cols = slice(j * LANES, (j + 1) * LANES)
        vb = v_ref[pl.ds(koff, band), cols]
        o = jnp.dot(p_ref[j], vb, preferred_element_type=F32) * inv_l[j]
        outs.append(jnp.where(first, o[:GRID_W], o[GRID_W:]))
    o_all = jnp.concatenate(outs, axis=1)
    o_ref[...] = _rms(o_all, g_ref[...]).astype(o_ref.dtype)


def _nat_row_start(r, rows):
    return jnp.clip(r - NAT_WIN_R // 2, 0, rows - NAT_WIN_R) - r + (NAT_WIN_R - 1)


def _nat(proj, bias_tbl, gain, *, batch, seq):
    rows = seq // GRID_W
    assert rows >= NAT_WIN_R
    t = proj.shape[0]
    pairs = bias_tbl.shape[1]
    width = pairs * LANES
    return pl.pallas_call(
        functools.partial(_nat_kernel, rows=rows, pairs=pairs),
        grid=(batch, rows),
        in_specs=[
            pl.BlockSpec((GRID_W, width), lambda b, r: (b * rows + r, 0)),
            pl.BlockSpec((seq, width), lambda b, r: (b, 1)),
            pl.BlockSpec((seq, width), lambda b, r: (b, 2)),
            pl.BlockSpec((1, pairs, 2 * GRID_W, NAT_WIN_R * GRID_W), lambda b, r: (_nat_row_start(r, rows), 0, 0, 0)),
            pl.BlockSpec((1, width), lambda b, r: (0, 0)),
        ],
        out_specs=pl.BlockSpec((GRID_W, width), lambda b, r: (b * rows + r, 0)),
        out_shape=jax.ShapeDtypeStruct((t, width), BF16),
        scratch_shapes=[
            pltpu.VMEM((pairs, 2 * GRID_W, NAT_WIN_R * GRID_W), F32),
            pltpu.VMEM((pairs, 2 * GRID_W, NAT_WIN_R * GRID_W), BF16),
        ],
        compiler_params=_params(("parallel", "arbitrary"), 48),
        name="nat_attention",
    )(proj, proj, proj, bias_tbl, gain.reshape(1, width).astype(F32))


def _nat_bias_tables(rpb):
    n_l, heads = rpb.shape[0], rpb.shape[1]
    qc = np.arange(GRID_W)[:, None]
    kc = np.arange(GRID_W)[None, :]
    c_start = np.clip(qc - NAT_WIN_C // 2, 0, GRID_W - NAT_WIN_C)
    valid = (kc >= c_start) & (kc < c_start + NAT_WIN_C)
    dc_idx = np.clip(kc - qc, -(NAT_WIN_C - 1), NAT_WIN_C - 1) + (NAT_WIN_C - 1)
    tbl = rpb.astype(F32)[:, :, :, dc_idx]
    tbl = jnp.where(jnp.asarray(valid), tbl, NEG_BIG)
    starts = np.arange(NAT_WIN_R)[:, None] + np.arange(NAT_WIN_R)[None, :]
    tbl = tbl[:, :, starts]
    tbl = jnp.transpose(tbl, (0, 2, 1, 4, 3, 5))
    return tbl.reshape(n_l, NAT_WIN_R, heads // 2, 2 * GRID_W, NAT_WIN_R * GRID_W)


def _top_rows_ranked(x, dst_ref, k):
    rank = jnp.full(x.shape, float(k), F32)
    for i in range(k):
        m = jnp.max(x, axis=0, keepdims=True)
        dst_ref[i:i + 1, :] = m
        hit = x == m
        rank = jnp.where(hit, float(i), rank)
        x = jnp.where(hit, -jnp.inf, x)
    return rank


def _top2_rows(x1, x2, dst1_ref, dst2_ref, k):
    rank2 = jnp.full(x2.shape, float(k), F32)
    for i in range(k):
        m1 = jnp.max(x1, axis=0, keepdims=True)
        m2 = jnp.max(x2, axis=0, keepdims=True)
        dst1_ref[i:i + 1, :] = m1
        dst2_ref[i:i + 1, :] = m2
        hit2 = x2 == m2
        x1 = jnp.where(x1 == m1, -jnp.inf, x1)
        rank2 = jnp.where(hit2, float(i), rank2)
        x2 = jnp.where(hit2, -jnp.inf, x2)
    return rank2


def _dup_bf16_bits(x):
    bits = pltpu.bitcast(x.astype(BF16).astype(F32), jnp.uint32)
    return bits | (bits >> 16)


def _route_kernel(ht_ref, wq_ref, keys_ref, cnt_ref, c1_ref, r2_ref, e2_ref, a_ref, b_ref, cand_ref, top_ref):
    nk = keys_ref.shape[1]
    tm = ht_ref.shape[1]
    qt = jnp.dot(wq_ref[...], ht_ref[...], preferred_element_type=F32)
    half = qt.shape[0] // 2
    s1_all = jnp.dot(keys_ref[0], qt[:half].astype(BF16), preferred_element_type=F32)
    s2_all = jnp.dot(keys_ref[1], qt[half:].astype(BF16), preferred_element_type=F32)
    for c in range(tm // LANES):
        lanes = slice(c * LANES, (c + 1) * LANES)
        s1, s2 = s1_all[:, lanes], s2_all[:, lanes]
        rank2 = _top2_rows(s1, s2, a_ref, b_ref, PEER_TOPK)
        offs, off = [], 0
        for r in range(PEER_TOPK):
            n_r = PEER_TOPK // (r + 1)
            cand_ref[off:off + n_r, :] = a_ref[r:r + 1, :] + b_ref[0:n_r, :]
            offs.append((off, n_r))
            off += n_r
        if off < cand_ref.shape[0]:
            cand_ref[off:, :] = jnp.full((cand_ref.shape[0] - off, LANES), -jnp.inf, F32)
        taken = _top_rows_ranked(cand_ref[...], top_ref, PEER_TOPK) < float(PEER_TOPK)
        taken = taken.astype(F32)
        top = top_ref[...]
        z = jnp.sum(jnp.exp(top - top[0:1]), axis=0, keepdims=True)
        cnt = jnp.zeros((nk, LANES), F32)
        for r, (off, n_r) in enumerate(offs):
            cnt_r = jnp.sum(taken[off:off + n_r], axis=0, keepdims=True)
            cnt = jnp.where(s1 == a_ref[r:r + 1, :], cnt_r, cnt)
        c1 = jnp.exp(s1 - a_ref[0:1, :]) / z
        e2 = jnp.exp(s2 - b_ref[0:1, :])
        cnt_ref[0, :, lanes] = _dup_bf16_bits(cnt)
        c1_ref[0, :, lanes] = _dup_bf16_bits(c1)
        r2_ref[0, :, lanes] = rank2.astype(BF16)
        e2_ref[0, :, lanes] = e2.astype(BF16)


def _route(ht, wqt, keys, *, tm=512):
    d, t = ht.shape
    heads = PEER_HEADS
    dq2 = wqt.shape[0] // heads
    nk = keys.shape[1]
    tm = min(tm, t)
    n_cand = -(-sum(PEER_TOPK // (r + 1) for r in range(PEER_TOPK)) // 8) * 8
    row_shape = jax.ShapeDtypeStruct((heads, nk, t), jnp.uint32)
    row_spec = pl.BlockSpec((1, nk, tm), lambda i, h: (h, 0, i))
    tile_shape = jax.ShapeDtypeStruct((heads, nk, t), BF16)
    tile_spec = pl.BlockSpec((1, nk, tm), lambda i, h: (h, 0, i))
    return pl.pallas_call(
        _route_kernel,
        grid=(t // tm, heads),
        in_specs=[
            pl.BlockSpec((d, tm), lambda i, h: (0, i)),
            pl.BlockSpec((dq2, d), lambda i, h: (h, 0)),
            pl.BlockSpec(keys.shape, lambda i, h: (0, 0, 0)),
        ],
        out_specs=[row_spec, row_spec, tile_spec, tile_spec],
        out_shape=[row_shape, row_shape, tile_shape, tile_shape],
        scratch_shapes=[
            pltpu.VMEM((PEER_TOPK, LANES), F32),
            pltpu.VMEM((PEER_TOPK, LANES), F32),
            pltpu.VMEM((n_cand, LANES), F32),
            pltpu.VMEM((PEER_TOPK, LANES), F32),
        ],
        compiler_params=_params(("parallel", "arbitrary"), 40),
        name="peer_route",
    )(ht, wqt, keys)


def _gelu(x):
    return 0.5 * x * (1.0 + lax.erf(x * 0.7071067811865476))


GATE_LANES = 256
GATE_KGROUP = 2


def _row_as_bf16(row):
    return pltpu.bitcast(jnp.broadcast_to(row, (8, row.shape[1])), BF16)


def _build_gates(r2_ref, e2_ref, cnt_ref, c1_ref, row0, put, *, n1, heads):
    nk = r2_ref.shape[1]
    tm = r2_ref.shape[2]
    n_sub = nk // SUB16
    for k0 in range(0, n1, GATE_KGROUP):
        ks = range(k0, min(k0 + GATE_KGROUP, n1))
        for c in range(tm // GATE_LANES):
            lanes = slice(c * GATE_LANES, (c + 1) * GATE_LANES)
            g = {(k, s): jnp.zeros((SUB16, GATE_LANES), BF16) for k in ks for s in range(n_sub)}
            for h in range(heads):
                cnt = {k: _row_as_bf16(cnt_ref[h, row0 + k:row0 + k + 1, lanes]) for k in ks}
                c1 = {k: _row_as_bf16(c1_ref[h, row0 + k:row0 + k + 1, lanes]) for k in ks}
                for s in range(n_sub):
                    rows = slice(s * SUB16, (s + 1) * SUB16)
                    r2 = r2_ref[h, rows, lanes]
                    e2 = e2_ref[h, rows, lanes]
                    for k in ks:
                        g[k, s] = g[k, s] + jnp.where(r2 < cnt[k], e2, jnp.zeros_like(e2)) * c1[k]
            for k in ks:
                for s in range(n_sub):
                    put(k, s, c, g[k, s])


def _experts_kernel(ht_ref, u_ref, vt_ref, cnt_ref, c1_ref, cntn_ref, c1n_ref, r2_ref, e2_ref, x_ref, o_ref,
                    acc_ref, ga_ref, a_ref, *, n1, heads):
    j = pl.program_id(1)
    nk = r2_ref.shape[1]
    tm = ht_ref.shape[1]
    te = n1 * nk
    n_sub = nk // SUB16
    build = functools.partial(_build_gates, r2_ref, e2_ref, n1=n1, heads=heads)
    slot, next_slot = j % 2, (j + 1) % 2

    def put_slot(which):
        def put(k, s, c, tile):
            ga_ref[which, k * nk + s * SUB16:k * nk + (s + 1) * SUB16, c * GATE_LANES:(c + 1) * GATE_LANES] = tile
        return put

    @pl.when(j == 0)
    def _():
        acc_ref[...] = jnp.zeros_like(acc_ref)
        build(cnt_ref, c1_ref, 0, put_slot(0))

    act0 = jnp.dot(u_ref[0:te, :], ht_ref[...], preferred_element_type=F32)
    a_ref[0:te, :] = _gelu(act0).astype(BF16) * ga_ref[slot]
    build(cntn_ref, c1n_ref, 0, put_slot(next_slot))
    tiles = {}
    build(cnt_ref, c1_ref, n1, lambda k, s, c, tile: tiles.__setitem__((k, s, c), tile))
    gb = jnp.concatenate(
        [jnp.concatenate([tiles[k, s, c] for c in range(tm // GATE_LANES)], axis=1)
         for k in range(n1) for s in range(n_sub)], axis=0)
    act1 = jnp.dot(u_ref[te:2 * te, :], ht_ref[...], preferred_element_type=F32)
    a_ref[te:2 * te, :] = _gelu(act1).astype(BF16) * gb
    acc_ref[...] += jnp.dot(vt_ref[...], a_ref[...], preferred_element_type=F32)

    @pl.when(j == pl.num_programs(1) - 1)
    def _():
        o_ref[...] = x_ref[...] + acc_ref[...].T


def _experts(ht, u, vt, cnt, c1, r2, e2, x, *, tm=512, n1=4):
    d, t = ht.shape
    heads, nk, _ = cnt.shape
    n_e = u.shape[0]
    tm = min(tm, t)
    te = n1 * nk
    n_j = n_e // (2 * te)
    assert n_e == n_j * 2 * te and (2 * n1) % 8 == 0
    rows_now = pl.BlockSpec((heads, 2 * n1, tm), lambda i, j: (0, j, i))
    rows_next = pl.BlockSpec((heads, 2 * n1, tm), lambda i, j: (0, jnp.minimum(j + 1, n_j - 1), i))
    tile_spec = pl.BlockSpec((heads, nk, tm), lambda i, j: (0, 0, i))
    return pl.pallas_call(
        functools.partial(_experts_kernel, n1=n1, heads=heads),
        grid=(t // tm, n_j),
        in_specs=[
            pl.BlockSpec((d, tm), lambda i, j: (0, i)),
            pl.BlockSpec((2 * te, d), lambda i, j: (j, 0)),
            pl.BlockSpec((d, 2 * te), lambda i, j: (0, j)),
            rows_now, rows_now, rows_next, rows_next, tile_spec, tile_spec,
            pl.BlockSpec((tm, d), lambda i, j: (i, 0)),
        ],
        out_specs=pl.BlockSpec((tm, d), lambda i, j: (i, 0)),
        out_shape=jax.ShapeDtypeStruct((t, d), F32),
        scratch_shapes=[
            pltpu.VMEM((d, tm), F32),
            pltpu.VMEM((2, te, tm), BF16),
            pltpu.VMEM((2 * te, tm), BF16),
        ],
        compiler_params=_params(("parallel", "arbitrary"), 56),
        name="peer_experts",
    )(ht, u, vt, cnt, c1, cnt, c1, r2, e2, x)


def _swap_halves(w):
    half = w.shape[-1] // 2
    return jnp.concatenate([w[..., half:], w[..., :half]], axis=-1)


def _prep_w_in(w_in, q_lora, kv_lora, nat_width):
    o1 = q_lora
    o2 = o1 + kv_lora
    o3 = o2 + MLA_ROPE
    c_q, c_kv, k_pe, nat = w_in[..., :o1], w_in[..., o1:o2], w_in[..., o2:o3], w_in[..., o3:]
    assert nat.shape[-1] == 3 * nat_width
    k_sw = _swap_halves(k_pe)
    return jnp.concatenate([nat, c_q, c_kv, k_pe, k_pe, k_sw, k_sw], axis=-1).astype(BF16)


def _prep_w_uq(w_uq):
    n_l, k, _ = w_uq.shape
    w = w_uq.reshape(n_l, k, MLA_HEADS, MLA_NOPE + MLA_ROPE)
    nope = w[..., :MLA_NOPE].reshape(n_l, k, MLA_HEADS * MLA_NOPE)
    pe = w[..., MLA_NOPE:]
    return jnp.concatenate([nope, pe.reshape(n_l, k, -1), _swap_halves(pe).reshape(n_l, k, -1)], axis=-1).astype(BF16)


def _prep_w_ukv(w_ukv):
    n_l, k, n = w_ukv.shape
    w = w_ukv.reshape(n_l, k, MLA_HEADS, n // MLA_HEADS)
    return jnp.concatenate([w[..., :MLA_NOPE].reshape(n_l, k, -1), w[..., MLA_NOPE:].reshape(n_l, k, -1)],
                           axis=-1).astype(BF16)


def _rope_tables(seq):
    inv = ROPE_BASE ** (-jnp.arange(0, MLA_ROPE, 2, dtype=F32) / MLA_ROPE)
    ang = jnp.arange(seq, dtype=F32)[:, None] * inv[None, :]
    cos, sin = jnp.cos(ang), jnp.sin(ang)
    return jnp.concatenate([cos, cos, cos, cos], axis=-1), jnp.concatenate([-sin, sin, -sin, sin], axis=-1)


def kernel(x, attn_norm, w_in, mla_q_norm, mla_w_uq, mla_kv_norm, mla_w_ukv, nat_rpb, mla_out_norm, nat_out_norm,
           w_out, ffn_norm, peer_w_q, peer_sub_keys, peer_u, peer_v, final_norm):
    batch, seq, d = x.shape
    depth = w_in.shape[0]
    q_lora, kv_lora = mla_q_norm.shape[1], mla_kv_norm.shape[1]
    nat_width = nat_rpb.shape[1] * NAT_HEAD_DIM
    mla_width = mla_out_norm.shape[1]
    assert mla_w_ukv.shape[2] == MLA_HEADS * 2 * LANES and mla_width == MLA_HEADS * LANES
    assert nat_width == mla_width and q_lora % LANES == 0 and kv_lora % LANES == 0
    t = batch * seq

    w1 = _prep_w_in(w_in, q_lora, kv_lora, nat_width)
    cq_col = 3 * nat_width // q_lora
    ckv_col = (3 * nat_width + q_lora) // kv_lora
    kpe_col = (3 * nat_width + q_lora + kv_lora) // LANES
    wq = _prep_w_uq(mla_w_uq)
    wkv = _prep_w_ukv(mla_w_ukv)
    wo = w_out.astype(BF16)
    wpq_t = jnp.swapaxes(peer_w_q, 1, 2).astype(BF16)
    keys = peer_sub_keys.astype(BF16)
    u = peer_u.astype(BF16)
    vt = jnp.swapaxes(peer_v, 1, 2).astype(BF16)
    bias_tbl = _nat_bias_tables(nat_rpb)
    cc, ss = _rope_tables(seq)

    xf = x.reshape(t, d)
    for l in range(depth):
        proj = _matmul([(xf, 0, d)], w1[l], out_dtype=BF16, gain=attn_norm[l], tn=1024, name="proj_in")
        q = _matmul([(proj, cq_col, q_lora)], wq[l], out_dtype=BF16, gain=mla_q_norm[l], tn=1024, name="mla_q_up")
        kv = _matmul([(proj, ckv_col, kv_lora)], wkv[l], out_dtype=BF16, gain=mla_kv_norm[l], tn=1024, name="mla_kv_up")
        mla_o = _mla(q, kv, proj, cc, ss, mla_out_norm[l], batch=batch, seq=seq, kpe_col=kpe_col)
        nat_o = _nat(proj, bias_tbl[l], nat_out_norm[l], batch=batch, seq=seq)
        xf = _matmul([(mla_o, 0, mla_width), (nat_o, 0, nat_width)], wo[l], out_dtype=F32, res=xf, name="mix_out")
        ht = _rmsnorm(xf, ffn_norm[l], out_dtype=BF16, transpose=True)
        cnt, c1, r2, e2 = _route(ht, wpq_t[l], keys[l])
        xf = _experts(ht, u[l], vt[l], cnt, c1, r2, e2, xf)
    out = _rmsnorm(xf, final_norm, out_dtype=F32)
    return out.reshape(batch, seq, d)
```

```python
import functools

import numpy as np
import jax
import jax.numpy as jnp
from jax import lax
from jax.experimental import pallas as pl
from jax.experimental.pallas import tpu as pltpu

F32 = jnp.float32
BF16 = jnp.bfloat16

RMS_EPS = 1e-6
GRID_W = 64
MLA_HEADS = 8
MLA_NOPE = 128
MLA_ROPE = 64
ROPE_BASE = 10000.0
NAT_HEAD_DIM = 64
NAT_WIN_R = 8
NAT_WIN_C = 16
PEER_HEADS = 8
PEER_TOPK = 16
LANES = 128
SUB16 = 16
NEG_BIG = -1e30
MIB = 1024 * 1024


def _params(sem, vmem_mib):
    return pltpu.CompilerParams(dimension_semantics=sem, vmem_limit_bytes=vmem_mib * MIB)


def _rms(x, g):
    ms = jnp.mean(x * x, axis=-1, keepdims=True)
    return x * lax.rsqrt(ms + RMS_EPS) * g


def _rmsnorm_kernel(x_ref, g_ref, o_ref, *, transpose):
    y = _rms(x_ref[...].astype(F32), g_ref[...])
    if transpose:
        y = y.T
    o_ref[...] = y.astype(o_ref.dtype)


def _rmsnorm(x, g, *, out_dtype, transpose=False, tm=512):
    t, d = x.shape
    tm = min(tm, t)
    if transpose:
        out_shape = jax.ShapeDtypeStruct((d, t), out_dtype)
        out_spec = pl.BlockSpec((d, tm), lambda i: (0, i))
    else:
        out_shape = jax.ShapeDtypeStruct((t, d), out_dtype)
        out_spec = pl.BlockSpec((tm, d), lambda i: (i, 0))
    return pl.pallas_call(
        functools.partial(_rmsnorm_kernel, transpose=transpose),
        grid=(t // tm,),
        in_specs=[pl.BlockSpec((tm, d), lambda i: (i, 0)), pl.BlockSpec((1, d), lambda i: (0, 0))],
        out_specs=out_spec,
        out_shape=out_shape,
        compiler_params=_params(("parallel",), 40),
        name="rmsnorm_t" if transpose else "rmsnorm",
    )(x, g.reshape(1, d).astype(F32))


def _mm_kernel(*refs, n_a, norm, has_res):
    a_refs = refs[:n_a]
    pos = n_a
    g_ref = refs[pos] if norm else None
    pos += int(norm)
    w_ref = refs[pos]
    pos += 1
    res_ref = refs[pos] if has_res else None
    pos += int(has_res)
    o_ref, an_ref = refs[pos], refs[pos + 1]

    @pl.when(pl.program_id(1) == 0)
    def _():
        parts = [r[...] for r in a_refs]
        a = parts[0] if n_a == 1 else jnp.concatenate(parts, axis=1)
        if norm:
            a = _rms(a.astype(F32), g_ref[...])
        an_ref[...] = a.astype(BF16)

    acc = jnp.dot(an_ref[...], w_ref[...], preferred_element_type=F32)
    if has_res:
        acc = acc + res_ref[...]
    o_ref[...] = acc.astype(o_ref.dtype)


def _matmul(a_list, w, *, out_dtype, gain=None, res=None, tm=512, tn=512, name="matmul"):
    t = a_list[0][0].shape[0]
    k, n = w.shape
    assert k == sum(width for _, _, width in a_list)
    tm, tn = min(tm, t), min(tn, n)
    in_specs, args = [], []
    for arr, cb, width in a_list:
        in_specs.append(pl.BlockSpec((tm, width), lambda i, j, cb=cb: (i, cb)))
        args.append(arr)
    if gain is not None:
        in_specs.append(pl.BlockSpec((1, k), lambda i, j: (0, 0)))
        args.append(gain.reshape(1, k).astype(F32))
    in_specs.append(pl.BlockSpec((k, tn), lambda i, j: (0, j)))
    args.append(w)
    if res is not None:
        in_specs.append(pl.BlockSpec((tm, tn), lambda i, j: (i, j)))
        args.append(res)
    return pl.pallas_call(
        functools.partial(_mm_kernel, n_a=len(a_list), norm=gain is not None, has_res=res is not None),
        grid=(t // tm, n // tn),
        in_specs=in_specs,
        out_specs=pl.BlockSpec((tm, tn), lambda i, j: (i, j)),
        out_shape=jax.ShapeDtypeStruct((t, n), out_dtype),
        scratch_shapes=[pltpu.VMEM((tm, k), BF16)],
        compiler_params=_params(("parallel", "arbitrary"), 48),
        name=name,
    )(*args)


def _mla_kernel(q_ref, k_ref, v_ref, ka_ref, kb_ref, cck_ref, ssk_ref, ccq_ref, ssq_ref, g_ref, o_ref, kcat_ref,
                s_ref, p_ref, *, heads, scale):
    @pl.when(pl.program_id(1) == 0)
    def _():
        rk = ka_ref[...].astype(F32) * cck_ref[...] + kb_ref[...].astype(F32) * ssk_ref[...]
        rk = rk.astype(BF16)
        for h in range(heads):
            kcat_ref[h, :, 0:LANES] = k_ref[:, h * LANES:(h + 1) * LANES]
            kcat_ref[h, :, LANES:2 * LANES] = rk

    tq = q_ref.shape[0]
    pe0 = heads * MLA_NOPE
    sw0 = pe0 + (heads // 2) * LANES
    ccq, ssq = ccq_ref[...], ssq_ref[...]
    lane = lax.broadcasted_iota(jnp.int32, (tq, LANES), 1)

    def scores(h):
        p_lo = pe0 + (h // 2) * LANES
        s_lo = sw0 + (h // 2) * LANES
        rq = q_ref[:, p_lo:p_lo + LANES].astype(F32) * ccq + q_ref[:, s_lo:s_lo + LANES].astype(F32) * ssq
        keep = (lane < MLA_ROPE) if h % 2 == 0 else (lane >= MLA_ROPE)
        rq = jnp.where(keep, rq, 0.0).astype(BF16)
        qcat = jnp.concatenate([q_ref[:, h * LANES:(h + 1) * LANES], rq], axis=1)
        s_ref[h % 2] = lax.dot_general(qcat, kcat_ref[h], (((1,), (1,)), ((), ())), preferred_element_type=F32)

    def softmax(h):
        c = scale * 1.4426950408889634
        inv = []
        for r in range(tq // SUB16):
            rows = slice(r * SUB16, (r + 1) * SUB16)
            s = s_ref[h % 2, rows, :]
            m = jnp.max(s, axis=-1, keepdims=True)
            p = jnp.exp2((s - m) * c)
            inv.append(1.0 / jnp.sum(p, axis=-1, keepdims=True))
            p_ref[h % 2, rows, :] = p.astype(BF16)
        return jnp.concatenate(inv, axis=0)

    def values(h, inv_l):
        return jnp.dot(p_ref[h % 2], v_ref[:, h * LANES:(h + 1) * LANES], preferred_element_type=F32) * inv_l

    outs, inv_l = [], {}
    scores(0)
    for h in range(heads):
        if h + 1 < heads:
            scores(h + 1)
        inv_l[h] = softmax(h)
        if h >= 1:
            outs.append(values(h - 1, inv_l[h - 1]))
    outs.append(values(heads - 1, inv_l[heads - 1]))
    o_all = jnp.concatenate(outs, axis=1)
    o_ref[...] = _rms(o_all, g_ref[...]).astype(o_ref.dtype)


def _mla(q, kv, proj, cc, ss, gain, *, batch, seq, kpe_col, tq=256):
    heads = MLA_HEADS
    t = q.shape[0]
    tq = min(tq, seq)
    nq = seq // tq
    width = heads * LANES
    return pl.pallas_call(
        functools.partial(_mla_kernel, heads=heads, scale=float((MLA_NOPE + MLA_ROPE) ** -0.5)),
        grid=(batch, nq),
        in_specs=[
            pl.BlockSpec((tq, q.shape[1]), lambda b, i: (b * nq + i, 0)),
            pl.BlockSpec((seq, width), lambda b, i: (b, 0)),
            pl.BlockSpec((seq, width), lambda b, i: (b, 1)),
            pl.BlockSpec((seq, LANES), lambda b, i: (b, kpe_col)),
            pl.BlockSpec((seq, LANES), lambda b, i: (b, kpe_col + 1)),
            pl.BlockSpec((seq, LANES), lambda b, i: (0, 0)),
            pl.BlockSpec((seq, LANES), lambda b, i: (0, 0)),
            pl.BlockSpec((tq, LANES), lambda b, i: (i, 0)),
            pl.BlockSpec((tq, LANES), lambda b, i: (i, 0)),
            pl.BlockSpec((1, width), lambda b, i: (0, 0)),
        ],
        out_specs=pl.BlockSpec((tq, width), lambda b, i: (b * nq + i, 0)),
        out_shape=jax.ShapeDtypeStruct((t, width), BF16),
        scratch_shapes=[
            pltpu.VMEM((heads, seq, 2 * LANES), BF16),
            pltpu.VMEM((2, tq, seq), F32),
            pltpu.VMEM((2, tq, seq), BF16),
        ],
        compiler_params=_params(("parallel", "arbitrary"), 56),
        name="mla_attention",
    )(q, kv, kv, proj, proj, cc, ss, cc, ss, gain.reshape(1, width).astype(F32))


def _nat_kernel(q_ref, k_ref, v_ref, bias_ref, g_ref, o_ref, s_ref, p_ref, *, rows, pairs):
    r = pl.program_id(1)
    rs = jnp.clip(r - NAT_WIN_R // 2, 0, rows - NAT_WIN_R)
    koff = pl.multiple_of(rs * GRID_W, GRID_W)
    band = NAT_WIN_R * GRID_W
    lane = lax.broadcasted_iota(jnp.int32, (GRID_W, LANES), 1)
    first = lane < NAT_HEAD_DIM
    for j in range(pairs):
        cols = slice(j * LANES, (j + 1) * LANES)
        q2 = q_ref[:, cols] * jnp.asarray(NAT_HEAD_DIM ** -0.5, BF16)
        zero = jnp.zeros_like(q2)
        qq = jnp.concatenate([jnp.where(first, q2, zero), jnp.where(first, zero, q2)], axis=0)
        kb = k_ref[pl.ds(koff, band), cols]
        s_ref[j] = lax.dot_general(qq, kb, (((1,), (1,)), ((), ())), preferred_element_type=F32)
    inv_l = []
    for j in range(pairs):
        s = s_ref[j] + bias_ref[0, j]
        m = jnp.max(s, axis=-1, keepdims=True)
        p = jnp.exp(s - m)
        inv_l.append(1.0 / jnp.sum(p, axis=-1, keepdims=True))
        p_ref[j] = p.astype(BF16)
    outs = []
    for j in range(pairs):
        cols = slice(j * LANES, (j + 1) * LANES)
        vb = v_ref[pl.ds(koff, band), cols]
        o = jnp.dot(p_ref[j], vb, preferred_element_type=F32) * inv_l[j]
        outs.append(jnp.where(first, o[:GRID_W], o[GRID_W:]))
    o_all = jnp.concatenate(outs, axis=1)
    o_ref[...] = _rms(o_all, g_ref[...]).astype(o_ref.dtype)


def _nat_row_start(r, rows):
    return jnp.clip(r - NAT_WIN_R // 2, 0, rows - NAT_WIN_R) - r + (NAT_WIN_R - 1)


def _nat(proj, bias_tbl, gain, *, batch, seq):
    rows = seq // GRID_W
    assert rows >= NAT_WIN_R
    t = proj.shape[0]
    pairs = bias_tbl.shape[1]
    width = pairs * LANES
    return pl.pallas_call(
        functools.partial(_nat_kernel, rows=rows, pairs=pairs),
        grid=(batch, rows),
        in_specs=[
            pl.BlockSpec((GRID_W, width), lambda b, r: (b * rows + r, 0)),
            pl.BlockSpec((seq, width), lambda b, r: (b, 1)),
            pl.BlockSpec((seq, width), lambda b, r: (b, 2)),
            pl.BlockSpec((1, pairs, 2 * GRID_W, NAT_WIN_R * GRID_W), lambda b, r: (_nat_row_start(r, rows), 0, 0, 0)),
            pl.BlockSpec((1, width), lambda b, r: (0, 0)),
        ],
        out_specs=pl.BlockSpec((GRID_W, width), lambda b, r: (b * rows + r, 0)),
        out_shape=jax.ShapeDtypeStruct((t, width), BF16),
        scratch_shapes=[
            pltpu.VMEM((pairs, 2 * GRID_W, NAT_WIN_R * GRID_W), F32),
            pltpu.VMEM((pairs, 2 * GRID_W, NAT_WIN_R * GRID_W), BF16),
        ],
        compiler_params=_params(("parallel", "arbitrary"), 48),
        name="nat_attention",
    )(proj, proj, proj, bias_tbl, gain.reshape(1, width).astype(F32))


def _nat_bias_tables(rpb):
    n_l, heads = rpb.shape[0], rpb.shape[1]
    qc = np.arange(GRID_W)[:, None]
    kc = np.arange(GRID_W)[None, :]
    c_start = np.clip(qc - NAT_WIN_C // 2, 0, GRID_W - NAT_WIN_C)
    valid = (kc >= c_start) & (kc < c_start + NAT_WIN_C)
    dc_idx = np.clip(kc - qc, -(NAT_WIN_C - 1), NAT_WIN_C - 1) + (NAT_WIN_C - 1)
    tbl = rpb.astype(F32)[:, :, :, dc_idx]
    tbl = jnp.where(jnp.asarray(valid), tbl, NEG_BIG)
    starts = np.arange(NAT_WIN_R)[:, None] + np.arange(NAT_WIN_R)[None, :]
    tbl = tbl[:, :, starts]
    tbl = jnp.transpose(tbl, (0, 2, 1, 4, 3, 5))
    return tbl.reshape(n_l, NAT_WIN_R, heads // 2, 2 * GRID_W, NAT_WIN_R * GRID_W)


def _extract(x, rows, exact):
    m = jnp.max(x, axis=0, keepdims=True)
    hit = x == m
    if exact:
        first = jnp.min(jnp.where(hit, rows, float(x.shape[0])), axis=0, keepdims=True)
        hit = rows == first
    return m, hit


def _row_index(shape):
    return lax.broadcasted_iota(jnp.int32, shape, 0).astype(F32)


def _top_rows_ranked(x, dst_ref, k, exact):
    rows = _row_index(x.shape) if exact else None
    rank = jnp.full(x.shape, float(k), F32)
    for i in range(k):
        m, hit = _extract(x, rows, exact)
        dst_ref[i:i + 1, :] = m
        rank = jnp.where(hit, float(i), rank)
        x = jnp.where(hit, -jnp.inf, x)
    return rank, x


def _top2_rows(x1, x2, dst1_ref, dst2_ref, k, exact):
    rows = _row_index(x1.shape) if exact else None
    rank1 = jnp.full(x1.shape, float(k), F32) if exact else None
    rank2 = jnp.full(x2.shape, float(k), F32)
    for i in range(k):
        m1, hit1 = _extract(x1, rows, exact)
        m2, hit2 = _extract(x2, rows, exact)
        dst1_ref[i:i + 1, :] = m1
        dst2_ref[i:i + 1, :] = m2
        if exact:
            rank1 = jnp.where(hit1, float(i), rank1)
        x1 = jnp.where(hit1, -jnp.inf, x1)
        rank2 = jnp.where(hit2, float(i), rank2)
        x2 = jnp.where(hit2, -jnp.inf, x2)
    return rank1, rank2, x1, x2


def _count_removed(x):
    return jnp.sum(jnp.where(x == -jnp.inf, 1.0, 0.0), axis=0, keepdims=True)


def _dup_bf16_bits(x):
    bits = pltpu.bitcast(x.astype(BF16).astype(F32), jnp.uint32)
    return bits | (bits >> 16)


def _route_group(s1, s2, a_ref, b_ref, cand_ref, top_ref, exact):
    nk = s1.shape[0]
    rank1, rank2, x1, x2 = _top2_rows(s1, s2, a_ref, b_ref, PEER_TOPK, exact)
    offs, off = [], 0
    for r in range(PEER_TOPK):
        n_r = PEER_TOPK // (r + 1)
        cand_ref[off:off + n_r, :] = a_ref[r:r + 1, :] + b_ref[0:n_r, :]
        offs.append((off, n_r))
        off += n_r
    n_pad = cand_ref.shape[0] - off
    if n_pad:
        cand_ref[off:, :] = jnp.full((n_pad, LANES), -jnp.inf, F32)
    crank, xc = _top_rows_ranked(cand_ref[...], top_ref, PEER_TOPK, exact)
    taken = jnp.where(crank < float(PEER_TOPK), 1.0, 0.0)
    top = top_ref[...]
    z = jnp.sum(jnp.exp(top - top[0:1]), axis=0, keepdims=True)
    cnt = jnp.zeros((nk, LANES), F32)
    for r, (off, n_r) in enumerate(offs):
        cnt_r = jnp.sum(taken[off:off + n_r], axis=0, keepdims=True)
        is_r = (rank1 == float(r)) if exact else (s1 == a_ref[r:r + 1, :])
        cnt = jnp.where(is_r, cnt_r, cnt)
    c1 = jnp.exp(s1 - a_ref[0:1, :]) / z
    e2 = jnp.exp(s2 - b_ref[0:1, :])
    ties = None
    if not exact:
        k = float(PEER_TOPK)
        ties = ((_count_removed(x1) > k) | (_count_removed(x2) > k) | (_count_removed(xc) > k + n_pad))
    return cnt, c1, rank2, e2, ties


def _route_kernel(ht_ref, wq_ref, keys_ref, cnt_ref, c1_ref, r2_ref, e2_ref, a_ref, b_ref, cand_ref, top_ref):
    tm = ht_ref.shape[1]
    qt = jnp.dot(wq_ref[...], ht_ref[...], preferred_element_type=F32)
    half = qt.shape[0] // 2
    s1_all = jnp.dot(keys_ref[0], qt[:half].astype(BF16), preferred_element_type=F32)
    s2_all = jnp.dot(keys_ref[1], qt[half:].astype(BF16), preferred_element_type=F32)

    def store(c, cnt, c1, rank2, e2):
        lanes = slice(c * LANES, (c + 1) * LANES)
        cnt_ref[0, :, lanes] = _dup_bf16_bits(cnt)
        c1_ref[0, :, lanes] = _dup_bf16_bits(c1)
        r2_ref[0, :, lanes] = rank2.astype(BF16)
        e2_ref[0, :, lanes] = e2.astype(BF16)

    def group(c, exact):
        lanes = slice(c * LANES, (c + 1) * LANES)
        return _route_group(s1_all[:, lanes], s2_all[:, lanes], a_ref, b_ref, cand_ref, top_ref, exact)

    any_tie = jnp.zeros((1, LANES), F32)
    for c in range(tm // LANES):
        cnt, c1, rank2, e2, ties = group(c, exact=False)
        store(c, cnt, c1, rank2, e2)
        any_tie = jnp.maximum(any_tie, jnp.where(ties, 1.0, 0.0))

    @pl.when(jnp.max(any_tie) > 0.0)
    def _():
        for c in range(tm // LANES):
            store(c, *group(c, exact=True)[:4])


def _route(ht, wqt, keys, *, tm=512):
    d, t = ht.shape
    heads = PEER_HEADS
    dq2 = wqt.shape[0] // heads
    nk = keys.shape[1]
    tm = min(tm, t)
    n_cand = -(-sum(PEER_TOPK // (r + 1) for r in range(PEER_TOPK)) // 8) * 8
    row_shape = jax.ShapeDtypeStruct((heads, nk, t), jnp.uint32)
    row_spec = pl.BlockSpec((1, nk, tm), lambda i, h: (h, 0, i))
    tile_shape = jax.ShapeDtypeStruct((heads, nk, t), BF16)
    tile_spec = pl.BlockSpec((1, nk, tm), lambda i, h: (h, 0, i))
    return pl.pallas_call(
        _route_kernel,
        grid=(t // tm, heads),
        in_specs=[
            pl.BlockSpec((d, tm), lambda i, h: (0, i)),
            pl.BlockSpec((dq2, d), lambda i, h: (h, 0)),
            pl.BlockSpec(keys.shape, lambda i, h: (0, 0, 0)),
        ],
        out_specs=[row_spec, row_spec, tile_spec, tile_spec],
        out_shape=[row_shape, row_shape, tile_shape, tile_shape],
        scratch_shapes=[
            pltpu.VMEM((PEER_TOPK, LANES), F32),
            pltpu.VMEM((PEER_TOPK, LANES), F32),
            pltpu.VMEM((n_cand, LANES), F32),
            pltpu.VMEM((PEER_TOPK, LANES), F32),
        ],
        compiler_params=_params(("parallel", "arbitrary"), 40),
        name="peer_route",
    )(ht, wqt, keys)


def _gelu(x):
    return 0.5 * x * (1.0 + lax.erf(x * 0.7071067811865476))


GATE_LANES = 256
GATE_KGROUP = 2


def _row_as_bf16(row):
    return pltpu.bitcast(jnp.broadcast_to(row, (8, row.shape[1])), BF16)


def _build_gates(r2_ref, e2_ref, cnt_ref, c1_ref, row0, put, *, n1, heads):
    nk = r2_ref.shape[1]
    tm = r2_ref.shape[2]
    n_sub = nk // SUB16
    for k0 in range(0, n1, GATE_KGROUP):
        ks = range(k0, min(k0 + GATE_KGROUP, n1))
        for c in range(tm // GATE_LANES):
            lanes = slice(c * GATE_LANES, (c + 1) * GATE_LANES)
            g = {(k, s): jnp.zeros((SUB16, GATE_LANES), BF16) for k in ks for s in range(n_sub)}
            for h in range(heads):
                cnt = {k: _row_as_bf16(cnt_ref[h, row0 + k:row0 + k + 1, lanes]) for k in ks}
                c1 = {k: _row_as_bf16(c1_ref[h, row0 + k:row0 + k + 1, lanes]) for k in ks}
                for s in range(n_sub):
                    rows = slice(s * SUB16, (s + 1) * SUB16)
                    r2 = r2_ref[h, rows, lanes]
                    e2 = e2_ref[h, rows, lanes]
                    for k in ks:
                        g[k, s] = g[k, s] + jnp.where(r2 < cnt[k], e2, jnp.zeros_like(e2)) * c1[k]
            for k in ks:
                for s in range(n_sub):
                    put(k, s, c, g[k, s])


def _experts_kernel(ht_ref, u_ref, vt_ref, cnt_ref, c1_ref, cntn_ref, c1n_ref, r2_ref, e2_ref, x_ref, o_ref,
                    acc_ref, ga_ref, a_ref, *, n1, heads):
    j = pl.program_id(1)
    nk = r2_ref.shape[1]
    tm = ht_ref.shape[1]
    te = n1 * nk
    n_sub = nk // SUB16
    build = functools.partial(_build_gates, r2_ref, e2_ref, n1=n1, heads=heads)
    slot, next_slot = j % 2, (j + 1) % 2

    def put_slot(which):
        def put(k, s, c, tile):
            ga_ref[which, k * nk + s * SUB16:k * nk + (s + 1) * SUB16, c * GATE_LANES:(c + 1) * GATE_LANES] = tile
        return put

    @pl.when(j == 0)
    def _():
        acc_ref[...] = jnp.zeros_like(acc_ref)
        build(cnt_ref, c1_ref, 0, put_slot(0))

    act0 = jnp.dot(u_ref[0:te, :], ht_ref[...], preferred_element_type=F32)
    a_ref[0:te, :] = _gelu(act0).astype(BF16) * ga_ref[slot]
    build(cntn_ref, c1n_ref, 0, put_slot(next_slot))
    tiles = {}
    build(cnt_ref, c1_ref, n1, lambda k, s, c, tile: tiles.__setitem__((k, s, c), tile))
    gb = jnp.concatenate(
        [jnp.concatenate([tiles[k, s, c] for c in range(tm // GATE_LANES)], axis=1)
         for k in range(n1) for s in range(n_sub)], axis=0)
    act1 = jnp.dot(u_ref[te:2 * te, :], ht_ref[...], preferred_element_type=F32)
    a_ref[te:2 * te, :] = _gelu(act1).astype(BF16) * gb
    acc_ref[...] += jnp.dot(vt_ref[...], a_ref[...], preferred_element_type=F32)

    @pl.when(j == pl.num_programs(1) - 1)
    def _():
        o_ref[...] = x_ref[...] + acc_ref[...].T


def _experts(ht, u, vt, cnt, c1, r2, e2, x, *, tm=512, n1=4):
    d, t = ht.shape
    heads, nk, _ = cnt.shape
    n_e = u.shape[0]
    tm = min(tm, t)
    te = n1 * nk
    n_j = n_e // (2 * te)
    assert n_e == n_j * 2 * te and (2 * n1) % 8 == 0
    rows_now = pl.BlockSpec((heads, 2 * n1, tm), lambda i, j: (0, j, i))
    rows_next = pl.BlockSpec((heads, 2 * n1, tm), lambda i, j: (0, jnp.minimum(j + 1, n_j - 1), i))
    tile_spec = pl.BlockSpec((heads, nk, tm), lambda i, j: (0, 0, i))
    return pl.pallas_call(
        functools.partial(_experts_kernel, n1=n1, heads=heads),
        grid=(t // tm, n_j),
        in_specs=[
            pl.BlockSpec((d, tm), lambda i, j: (0, i)),
            pl.BlockSpec((2 * te, d), lambda i, j: (j, 0)),
            pl.BlockSpec((d, 2 * te), lambda i, j: (0, j)),
            rows_now, rows_now, rows_next, rows_next, tile_spec, tile_spec,
            pl.BlockSpec((tm, d), lambda i, j: (i, 0)),
        ],
        out_specs=pl.BlockSpec((tm, d), lambda i, j: (i, 0)),
        out_shape=jax.ShapeDtypeStruct((t, d), F32),
        scratch_shapes=[
            pltpu.VMEM((d, tm), F32),
            pltpu.VMEM((2, te, tm), BF16),
            pltpu.VMEM((2 * te, tm), BF16),
        ],
        compiler_params=_params(("parallel", "arbitrary"), 56),
        name="peer_experts",
    )(ht, u, vt, cnt, c1, cnt, c1, r2, e2, x)


def _swap_halves(w):
    half = w.shape[-1] // 2
    return jnp.concatenate([w[..., half:], w[..., :half]], axis=-1)


def _prep_w_in(w_in, q_lora, kv_lora, nat_width):
    o1 = q_lora
    o2 = o1 + kv_lora
    o3 = o2 + MLA_ROPE
    c_q, c_kv, k_pe, nat = w_in[..., :o1], w_in[..., o1:o2], w_in[..., o2:o3], w_in[..., o3:]
    assert nat.shape[-1] == 3 * nat_width
    k_sw = _swap_halves(k_pe)
    return jnp.concatenate([nat, c_q, c_kv, k_pe, k_pe, k_sw, k_sw], axis=-1).astype(BF16)


def _prep_w_uq(w_uq):
    n_l, k, _ = w_uq.shape
    w = w_uq.reshape(n_l, k, MLA_HEADS, MLA_NOPE + MLA_ROPE)
    nope = w[..., :MLA_NOPE].reshape(n_l, k, MLA_HEADS * MLA_NOPE)
    pe = w[..., MLA_NOPE:]
    return jnp.concatenate([nope, pe.reshape(n_l, k, -1), _swap_halves(pe).reshape(n_l, k, -1)], axis=-1).astype(BF16)


def _prep_w_ukv(w_ukv):
    n_l, k, n = w_ukv.shape
    w = w_ukv.reshape(n_l, k, MLA_HEADS, n // MLA_HEADS)
    return jnp.concatenate([w[..., :MLA_NOPE].reshape(n_l, k, -1), w[..., MLA_NOPE:].reshape(n_l, k, -1)],
                           axis=-1).astype(BF16)


def _rope_tables(seq):
    inv = ROPE_BASE ** (-jnp.arange(0, MLA_ROPE, 2, dtype=F32) / MLA_ROPE)
    ang = jnp.arange(seq, dtype=F32)[:, None] * inv[None, :]
    cos, sin = jnp.cos(ang), jnp.sin(ang)
    return jnp.concatenate([cos, cos, cos, cos], axis=-1), jnp.concatenate([-sin, sin, -sin, sin], axis=-1)


def kernel(x, attn_norm, w_in, mla_q_norm, mla_w_uq, mla_kv_norm, mla_w_ukv, nat_rpb, mla_out_norm, nat_out_norm,
           w_out, ffn_norm, peer_w_q, peer_sub_keys, peer_u, peer_v, final_norm):
    batch, seq, d = x.shape
    depth = w_in.shape[0]
    q_lora, kv_lora = mla_q_norm.shape[1], mla_kv_norm.shape[1]
    nat_width = nat_rpb.shape[1] * NAT_HEAD_DIM
    mla_width = mla_out_norm.shape[1]
    assert mla_w_ukv.shape[2] == MLA_HEADS * 2 * LANES and mla_width == MLA_HEADS * LANES
    assert nat_width == mla_width and q_lora % LANES == 0 and kv_lora % LANES == 0
    t = batch * seq

    w1 = _prep_w_in(w_in, q_lora, kv_lora, nat_width)
    cq_col = 3 * nat_width // q_lora
    ckv_col = (3 * nat_width + q_lora) // kv_lora
    kpe_col = (3 * nat_width + q_lora + kv_lora) // LANES
    wq = _prep_w_uq(mla_w_uq)
    wkv = _prep_w_ukv(mla_w_ukv)
    wo = w_out.astype(BF16)
    wpq_t = jnp.swapaxes(peer_w_q, 1, 2).astype(BF16)
    keys = peer_sub_keys.astype(BF16)
    u = peer_u.astype(BF16)
    vt = jnp.swapaxes(peer_v, 1, 2).astype(BF16)
    bias_tbl = _nat_bias_tables(nat_rpb)
    cc, ss = _rope_tables(seq)

    xf = x.reshape(t, d)
    for l in range(depth):
        proj = _matmul([(xf, 0, d)], w1[l], out_dtype=BF16, gain=attn_norm[l], tn=2048, name="proj_in")
        q = _matmul([(proj, cq_col, q_lora)], wq[l], out_dtype=BF16, gain=mla_q_norm[l], tn=2048, name="mla_q_up")
        kv = _matmul([(proj, ckv_col, kv_lora)], wkv[l], out_dtype=BF16, gain=mla_kv_norm[l], tn=2048, name="mla_kv_up")
        mla_o = _mla(q, kv, proj, cc, ss, mla_out_norm[l], batch=batch, seq=seq, kpe_col=kpe_col)
        nat_o = _nat(proj, bias_tbl[l], nat_out_norm[l], batch=batch, seq=seq)
        xf = _matmul([(mla_o, 0, mla_width), (nat_o, 0, nat_width)], wo[l], out_dtype=F32, res=xf, tn=2048,
                     name="mix_out")
        ht = _rmsnorm(xf, ffn_norm[l], out_dtype=BF16, transpose=True)
        cnt, c1, r2, e2 = _route(ht, wpq_t[l], keys[l])
        xf = _experts(ht, u[l], vt[l], cnt, c1, r2, e2, xf)
    out = _rmsnorm(xf, final_norm, out_dtype=F32)
    return out.reshape(batch, seq, d)
```

```python
import functools

import numpy as np
import jax
import jax.numpy as jnp
from jax import lax
from jax.experimental import pallas as pl
from jax.experimental.pallas import tpu as pltpu

F32 = jnp.float32
BF16 = jnp.bfloat16

RMS_EPS = 1e-6
GRID_W = 64
MLA_HEADS = 8
MLA_NOPE = 128
MLA_ROPE = 64
ROPE_BASE = 10000.0
NAT_HEAD_DIM = 64
NAT_WIN_R = 8
NAT_WIN_C = 16
PEER_HEADS = 8
PEER_TOPK = 16
LANES = 128
SUB16 = 16
NEG_BIG = -1e30
MIB = 1024 * 1024


def _params(sem, vmem_mib):
    return pltpu.CompilerParams(dimension_semantics=sem, vmem_limit_bytes=vmem_mib * MIB)


def _rms(x, g):
    ms = jnp.mean(x * x, axis=-1, keepdims=True)
    return x * lax.rsqrt(ms + RMS_EPS) * g


def _rmsnorm_kernel(x_ref, g_ref, o_ref, *, transpose):
    y = _rms(x_ref[...].astype(F32), g_ref[...])
    if transpose:
        y = y.T
    o_ref[...] = y.astype(o_ref.dtype)


def _rmsnorm(x, g, *, out_dtype, transpose=False, tm=512):
    t, d = x.shape
    tm = min(tm, t)
    if transpose:
        out_shape = jax.ShapeDtypeStruct((d, t), out_dtype)
        out_spec = pl.BlockSpec((d, tm), lambda i: (0, i))
    else:
        out_shape = jax.ShapeDtypeStruct((t, d), out_dtype)
        out_spec = pl.BlockSpec((tm, d), lambda i: (i, 0))
    return pl.pallas_call(
        functools.partial(_rmsnorm_kernel, transpose=transpose),
        grid=(t // tm,),
        in_specs=[pl.BlockSpec((tm, d), lambda i: (i, 0)), pl.BlockSpec((1, d), lambda i: (0, 0))],
        out_specs=out_spec,
        out_shape=out_shape,
        compiler_params=_params(("parallel",), 40),
        name="rmsnorm_t" if transpose else "rmsnorm",
    )(x, g.reshape(1, d).astype(F32))


def _mm_kernel(*refs, n_a, norm, has_res):
    a_refs = refs[:n_a]
    pos = n_a
    g_ref = refs[pos] if norm else None
    pos += int(norm)
    w_ref = refs[pos]
    pos += 1
    res_ref = refs[pos] if has_res else None
    pos += int(has_res)
    o_ref, an_ref = refs[pos], refs[pos + 1]

    @pl.when(pl.program_id(1) == 0)
    def _():
        parts = [r[...] for r in a_refs]
        a = parts[0] if n_a == 1 else jnp.concatenate(parts, axis=1)
        if norm:
            a = _rms(a.astype(F32), g_ref[...])
        an_ref[...] = a.astype(BF16)

    acc = jnp.dot(an_ref[...], w_ref[...], preferred_element_type=F32)
    if has_res:
        acc = acc + res_ref[...]
    o_ref[...] = acc.astype(o_ref.dtype)


def _matmul(a_list, w, layer, *, out_dtype, gain=None, res=None, tm=512, tn=512, name="matmul"):
    t = a_list[0][0].shape[0]
    _, k, n = w.shape
    assert k == sum(width for _, _, width in a_list)
    tm, tn = min(tm, t), min(tn, n)
    in_specs, args = [], []
    for arr, cb, width in a_list:
        in_specs.append(pl.BlockSpec((tm, width), lambda i, j, cb=cb: (i, cb)))
        args.append(arr)
    if gain is not None:
        in_specs.append(pl.BlockSpec((1, k), lambda i, j: (0, 0)))
        args.append(gain.reshape(1, k).astype(F32))
    in_specs.append(pl.BlockSpec((None, k, tn), lambda i, j: (layer, 0, j)))
    args.append(w)
    if res is not None:
        in_specs.append(pl.BlockSpec((tm, tn), lambda i, j: (i, j)))
        args.append(res)
    return pl.pallas_call(
        functools.partial(_mm_kernel, n_a=len(a_list), norm=gain is not None, has_res=res is not None),
        grid=(t // tm, n // tn),
        in_specs=in_specs,
        out_specs=pl.BlockSpec((tm, tn), lambda i, j: (i, j)),
        out_shape=jax.ShapeDtypeStruct((t, n), out_dtype),
        scratch_shapes=[pltpu.VMEM((tm, k), BF16)],
        compiler_params=_params(("parallel", "arbitrary"), 48),
        name=name,
    )(*args)


def _mla_kernel(q_ref, k_ref, v_ref, ka_ref, kb_ref, cck_ref, ssk_ref, ccq_ref, ssq_ref, g_ref, o_ref, kcat_ref,
                s_ref, p_ref, *, heads, scale):
    @pl.when(pl.program_id(1) == 0)
    def _():
        rk = ka_ref[...].astype(F32) * cck_ref[...] + kb_ref[...].astype(F32) * ssk_ref[...]
        rk = rk.astype(BF16)
        for h in range(heads):
            kcat_ref[h, :, 0:LANES] = k_ref[:, h * LANES:(h + 1) * LANES]
            kcat_ref[h, :, LANES:2 * LANES] = rk

    tq = q_ref.shape[0]
    pe0 = heads * MLA_NOPE
    sw0 = pe0 + (heads // 2) * LANES
    ccq, ssq = ccq_ref[...], ssq_ref[...]
    lane = lax.broadcasted_iota(jnp.int32, (tq, LANES), 1)

    def scores(h):
        p_lo = pe0 + (h // 2) * LANES
        s_lo = sw0 + (h // 2) * LANES
        rq = q_ref[:, p_lo:p_lo + LANES].astype(F32) * ccq + q_ref[:, s_lo:s_lo + LANES].astype(F32) * ssq
        keep = (lane < MLA_ROPE) if h % 2 == 0 else (lane >= MLA_ROPE)
        rq = jnp.where(keep, rq, 0.0).astype(BF16)
        qcat = jnp.concatenate([q_ref[:, h * LANES:(h + 1) * LANES], rq], axis=1)
        s_ref[h % 2] = lax.dot_general(qcat, kcat_ref[h], (((1,), (1,)), ((), ())), preferred_element_type=F32)

    def softmax(h):
        c = scale * 1.4426950408889634
        inv = []
        for r in range(tq // SUB16):
            rows = slice(r * SUB16, (r + 1) * SUB16)
            s = s_ref[h % 2, rows, :]
            m = jnp.max(s, axis=-1, keepdims=True)
            p = jnp.exp2((s - m) * c)
            inv.append(1.0 / jnp.sum(p, axis=-1, keepdims=True))
            p_ref[h % 2, rows, :] = p.astype(BF16)
        return jnp.concatenate(inv, axis=0)

    def values(h, inv_l):
        return jnp.dot(p_ref[h % 2], v_ref[:, h * LANES:(h + 1) * LANES], preferred_element_type=F32) * inv_l

    outs, inv_l = [], {}
    scores(0)
    for h in range(heads):
        if h + 1 < heads:
            scores(h + 1)
        inv_l[h] = softmax(h)
        if h >= 1:
            outs.append(values(h - 1, inv_l[h - 1]))
    outs.append(values(heads - 1, inv_l[heads - 1]))
    o_all = jnp.concatenate(outs, axis=1)
    o_ref[...] = _rms(o_all, g_ref[...]).astype(o_ref.dtype)


def _mla(q, kv, proj, cc, ss, gain, *, batch, seq, kpe_col, tq=256):
    heads = MLA_HEADS
    t = q.shape[0]
    tq = min(tq, seq)
    nq = seq // tq
    width = heads * LANES
    return pl.pallas_call(
        functools.partial(_mla_kernel, heads=heads, scale=float((MLA_NOPE + MLA_ROPE) ** -0.5)),
        grid=(batch, nq),
        in_specs=[
            pl.BlockSpec((tq, q.shape[1]), lambda b, i: (b * nq + i, 0)),
            pl.BlockSpec((seq, width), lambda b, i: (b, 0)),
            pl.BlockSpec((seq, width), lambda b, i: (b, 1)),
            pl.BlockSpec((seq, LANES), lambda b, i: (b, kpe_col)),
            pl.BlockSpec((seq, LANES), lambda b, i: (b, kpe_col + 1)),
            pl.BlockSpec((seq, LANES), lambda b, i: (0, 0)),
            pl.BlockSpec((seq, LANES), lambda b, i: (0, 0)),
            pl.BlockSpec((tq, LANES), lambda b, i: (i, 0)),
            pl.BlockSpec((tq, LANES), lambda b, i: (i, 0)),
            pl.BlockSpec((1, width), lambda b, i: (0, 0)),
        ],
        out_specs=pl.BlockSpec((tq, width), lambda b, i: (b * nq + i, 0)),
        out_shape=jax.ShapeDtypeStruct((t, width), BF16),
        scratch_shapes=[
            pltpu.VMEM((heads, seq, 2 * LANES), BF16),
            pltpu.VMEM((2, tq, seq), F32),
            pltpu.VMEM((2, tq, seq), BF16),
        ],
        compiler_params=_params(("parallel", "arbitrary"), 56),
        name="mla_attention",
    )(q, kv, kv, proj, proj, cc, ss, cc, ss, gain.reshape(1, width).astype(F32))


def _nat_kernel(q_ref, k_ref, v_ref, bias_ref, g_ref, o_ref, s_ref, p_ref, *, rows, pairs):
    r = pl.program_id(1)
    rs = jnp.clip(r - NAT_WIN_R // 2, 0, rows - NAT_WIN_R)
    start = rs - r + (NAT_WIN_R - 1)
    koff = pl.multiple_of(rs * GRID_W, GRID_W)
    band = NAT_WIN_R * GRID_W
    lane = lax.broadcasted_iota(jnp.int32, (GRID_W, LANES), 1)
    first = lane < NAT_HEAD_DIM
    for j in range(pairs):
        cols = slice(j * LANES, (j + 1) * LANES)
        q2 = q_ref[:, cols] * jnp.asarray(NAT_HEAD_DIM ** -0.5, BF16)
        zero = jnp.zeros_like(q2)
        qq = jnp.concatenate([jnp.where(first, q2, zero), jnp.where(first, zero, q2)], axis=0)
        kb = k_ref[pl.ds(koff, band), cols]
        s_ref[j] = lax.dot_general(qq, kb, (((1,), (1,)), ((), ())), preferred_element_type=F32)
    inv_l = []
    for j in range(pairs):
        bias = jnp.concatenate([bias_ref[j, start + 2 * m] for m in range(NAT_WIN_R // 2)], axis=1)
        s = s_ref[j] + bias
        m = jnp.max(s, axis=-1, keepdims=True)
        p = jnp.exp(s - m)
        inv_l.append(1.0 / jnp.sum(p, axis=-1, keepdims=True))
        p_ref[j] = p.astype(BF16)
    outs = []
    for j in range(pairs):
        cols = slice(j * LANES, (j + 1) * LANES)
        vb = v_ref[pl.ds(koff, band), cols]
        o = jnp.dot(p_ref[j], vb, preferred_element_type=F32) * inv_l[j]
        outs.append(jnp.where(first, o[:GRID_W], o[GRID_W:]))
    o_all = jnp.concatenate(outs, axis=1)
    o_ref[...] = _rms(o_all, g_ref[...]).astype(o_ref.dtype)


def _nat(proj, bias_tbl, layer, gain, *, batch, seq):
    rows = seq // GRID_W
    assert rows >= NAT_WIN_R
    t = proj.shape[0]
    pairs, n_off = bias_tbl.shape[1], bias_tbl.shape[2]
    width = pairs * LANES
    return pl.pallas_call(
        functools.partial(_nat_kernel, rows=rows, pairs=pairs),
        grid=(batch, rows),
        in_specs=[
            pl.BlockSpec((GRID_W, width), lambda b, r: (b * rows + r, 0)),
            pl.BlockSpec((seq, width), lambda b, r: (b, 1)),
            pl.BlockSpec((seq, width), lambda b, r: (b, 2)),
            pl.BlockSpec((None, pairs, n_off, 2 * GRID_W, 2 * GRID_W), lambda b, r: (layer, 0, 0, 0, 0)),
            pl.BlockSpec((1, width), lambda b, r: (0, 0)),
        ],
        out_specs=pl.BlockSpec((GRID_W, width), lambda b, r: (b * rows + r, 0)),
        out_shape=jax.ShapeDtypeStruct((t, width), BF16),
        scratch_shapes=[
            pltpu.VMEM((pairs, 2 * GRID_W, NAT_WIN_R * GRID_W), F32),
            pltpu.VMEM((pairs, 2 * GRID_W, NAT_WIN_R * GRID_W), BF16),
        ],
        compiler_params=_params(("parallel", "arbitrary"), 48),
        name="nat_attention",
    )(proj, proj, proj, bias_tbl, gain.reshape(1, width).astype(F32))


def _nat_bias_tables(rpb):
    n_l, heads, n_off = rpb.shape[0], rpb.shape[1], rpb.shape[2]
    qc = np.arange(GRID_W)[:, None]
    kc = np.arange(GRID_W)[None, :]
    c_start = np.clip(qc - NAT_WIN_C // 2, 0, GRID_W - NAT_WIN_C)
    valid = (kc >= c_start) & (kc < c_start + NAT_WIN_C)
    dc_idx = np.clip(kc - qc, -(NAT_WIN_C - 1), NAT_WIN_C - 1) + (NAT_WIN_C - 1)
    tbl = rpb.astype(F32)[:, :, :, dc_idx]
    tbl = jnp.where(jnp.asarray(valid), tbl, NEG_BIG)
    tbl = tbl.reshape(n_l, heads // 2, 2, n_off, GRID_W, GRID_W)
    tbl = jnp.transpose(tbl, (0, 1, 3, 2, 4, 5)).reshape(n_l, heads // 2, n_off, 2 * GRID_W, GRID_W)
    return jnp.concatenate([tbl[:, :, :-1], tbl[:, :, 1:]], axis=-1)


def _extract(x, rows, exact):
    m = jnp.max(x, axis=0, keepdims=True)
    hit = x == m
    if exact:
        first = jnp.min(jnp.where(hit, rows, float(x.shape[0])), axis=0, keepdims=True)
        hit = rows == first
    return m, hit


def _row_index(shape):
    return lax.broadcasted_iota(jnp.int32, shape, 0).astype(F32)


def _top_rows_ranked(x, dst_ref, k, exact):
    rows = _row_index(x.shape) if exact else None
    rank = jnp.full(x.shape, float(k), F32)
    for i in range(k):
        m, hit = _extract(x, rows, exact)
        dst_ref[i:i + 1, :] = m
        rank = jnp.where(hit, float(i), rank)
        x = jnp.where(hit, -jnp.inf, x)
    return rank, x


def _top2_rows(x1, x2, dst1_ref, dst2_ref, k, exact):
    rows = _row_index(x1.shape) if exact else None
    rank1 = jnp.full(x1.shape, float(k), F32) if exact else None
    rank2 = jnp.full(x2.shape, float(k), F32)
    for i in range(k):
        m1, hit1 = _extract(x1, rows, exact)
        m2, hit2 = _extract(x2, rows, exact)
        dst1_ref[i:i + 1, :] = m1
        dst2_ref[i:i + 1, :] = m2
        if exact:
            rank1 = jnp.where(hit1, float(i), rank1)
        x1 = jnp.where(hit1, -jnp.inf, x1)
        rank2 = jnp.where(hit2, float(i), rank2)
        x2 = jnp.where(hit2, -jnp.inf, x2)
    return rank1, rank2, x1, x2


def _count_removed(x):
    return jnp.sum(jnp.where(x == -jnp.inf, 1.0, 0.0), axis=0, keepdims=True)


def _dup_bf16_bits(x):
    bits = pltpu.bitcast(x.astype(BF16).astype(F32), jnp.uint32)
    return bits | (bits >> 16)


def _route_group(s1, s2, a_ref, b_ref, cand_ref, top_ref, exact):
    nk = s1.shape[0]
    rank1, rank2, x1, x2 = _top2_rows(s1, s2, a_ref, b_ref, PEER_TOPK, exact)
    offs, off = [], 0
    for r in range(PEER_TOPK):
        n_r = PEER_TOPK // (r + 1)
        cand_ref[off:off + n_r, :] = a_ref[r:r + 1, :] + b_ref[0:n_r, :]
        offs.append((off, n_r))
        off += n_r
    n_pad = cand_ref.shape[0] - off
    if n_pad:
        cand_ref[off:, :] = jnp.full((n_pad, LANES), -jnp.inf, F32)
    crank, xc = _top_rows_ranked(cand_ref[...], top_ref, PEER_TOPK, exact)
    taken = jnp.where(crank < float(PEER_TOPK), 1.0, 0.0)
    top = top_ref[...]
    z = jnp.sum(jnp.exp(top - top[0:1]), axis=0, keepdims=True)
    cnt = jnp.zeros((nk, LANES), F32)
    for r, (off, n_r) in enumerate(offs):
        cnt_r = jnp.sum(taken[off:off + n_r], axis=0, keepdims=True)
        is_r = (rank1 == float(r)) if exact else (s1 == a_ref[r:r + 1, :])
        cnt = jnp.where(is_r, cnt_r, cnt)
    c1 = jnp.exp(s1 - a_ref[0:1, :]) / z
    e2 = jnp.exp(s2 - b_ref[0:1, :])
    ties = None
    if not exact:
        k = float(PEER_TOPK)
        ties = ((_count_removed(x1) > k) | (_count_removed(x2) > k) | (_count_removed(xc) > k + n_pad))
    return cnt, c1, rank2, e2, ties


def _route_kernel(ht_ref, wq_ref, keys_ref, cnt_ref, c1_ref, r2_ref, e2_ref, a_ref, b_ref, cand_ref, top_ref):
    tm = ht_ref.shape[1]
    qt = jnp.dot(wq_ref[...], ht_ref[...], preferred_element_type=F32)
    half = qt.shape[0] // 2
    s1_all = jnp.dot(keys_ref[0], qt[:half].astype(BF16), preferred_element_type=F32)
    s2_all = jnp.dot(keys_ref[1], qt[half:].astype(BF16), preferred_element_type=F32)

    def store(c, cnt, c1, rank2, e2):
        lanes = slice(c * LANES, (c + 1) * LANES)
        cnt_ref[0, :, lanes] = _dup_bf16_bits(cnt)
        c1_ref[0, :, lanes] = _dup_bf16_bits(c1)
        r2_ref[0, :, lanes] = rank2.astype(BF16)
        e2_ref[0, :, lanes] = e2.astype(BF16)

    def group(c, exact):
        lanes = slice(c * LANES, (c + 1) * LANES)
        return _route_group(s1_all[:, lanes], s2_all[:, lanes], a_ref, b_ref, cand_ref, top_ref, exact)

    any_tie = jnp.zeros((1, LANES), F32)
    for c in range(tm // LANES):
        cnt, c1, rank2, e2, ties = group(c, exact=False)
        store(c, cnt, c1, rank2, e2)
        any_tie = jnp.maximum(any_tie, jnp.where(ties, 1.0, 0.0))

    @pl.when(jnp.max(any_tie) > 0.0)
    def _():
        for c in range(tm // LANES):
            store(c, *group(c, exact=True)[:4])


def _route(ht, wqt, keys, layer, *, tm=512):
    d, t = ht.shape
    heads = PEER_HEADS
    dq2 = wqt.shape[1] // heads
    nk = keys.shape[2]
    tm = min(tm, t)
    n_cand = -(-sum(PEER_TOPK // (r + 1) for r in range(PEER_TOPK)) // 8) * 8
    row_shape = jax.ShapeDtypeStruct((heads, nk, t), jnp.uint32)
    row_spec = pl.BlockSpec((1, nk, tm), lambda i, h: (h, 0, i))
    tile_shape = jax.ShapeDtypeStruct((heads, nk, t), BF16)
    tile_spec = pl.BlockSpec((1, nk, tm), lambda i, h: (h, 0, i))
    return pl.pallas_call(
        _route_kernel,
        grid=(t // tm, heads),
        in_specs=[
            pl.BlockSpec((d, tm), lambda i, h: (0, i)),
            pl.BlockSpec((None, dq2, d), lambda i, h: (layer, h, 0)),
            pl.BlockSpec((None,) + keys.shape[1:], lambda i, h: (layer, 0, 0, 0)),
        ],
        out_specs=[row_spec, row_spec, tile_spec, tile_spec],
        out_shape=[row_shape, row_shape, tile_shape, tile_shape],
        scratch_shapes=[
            pltpu.VMEM((PEER_TOPK, LANES), F32),
            pltpu.VMEM((PEER_TOPK, LANES), F32),
            pltpu.VMEM((n_cand, LANES), F32),
            pltpu.VMEM((PEER_TOPK, LANES), F32),
        ],
        compiler_params=_params(("parallel", "arbitrary"), 40),
        name="peer_route",
    )(ht, wqt, keys)


def _gelu(x):
    return 0.5 * x * (1.0 + lax.erf(x * 0.7071067811865476))


GATE_LANES = 256
GATE_KGROUP = 2


def _row_as_bf16(row):
    return pltpu.bitcast(jnp.broadcast_to(row, (8, row.shape[1])), BF16)


def _build_gates(r2_ref, e2_ref, cnt_ref, c1_ref, row0, put, *, n1, heads):
    nk = r2_ref.shape[1]
    tm = r2_ref.shape[2]
    n_sub = nk // SUB16
    for k0 in range(0, n1, GATE_KGROUP):
        ks = range(k0, min(k0 + GATE_KGROUP, n1))
        for c in range(tm // GATE_LANES):
            lanes = slice(c * GATE_LANES, (c + 1) * GATE_LANES)
            g = {(k, s): jnp.zeros((SUB16, GATE_LANES), BF16) for k in ks for s in range(n_sub)}
            for h in range(heads):
                cnt = {k: _row_as_bf16(cnt_ref[h, row0 + k:row0 + k + 1, lanes]) for k in ks}
                c1 = {k: _row_as_bf16(c1_ref[h, row0 + k:row0 + k + 1, lanes]) for k in ks}
                for s in range(n_sub):
                    rows = slice(s * SUB16, (s + 1) * SUB16)
                    r2 = r2_ref[h, rows, lanes]
                    e2 = e2_ref[h, rows, lanes]
                    for k in ks:
                        g[k, s] = g[k, s] + jnp.where(r2 < cnt[k], e2, jnp.zeros_like(e2)) * c1[k]
            for k in ks:
                for s in range(n_sub):
                    put(k, s, c, g[k, s])


def _experts_kernel(ht_ref, u_ref, vt_ref, cnt_ref, c1_ref, cntn_ref, c1n_ref, r2_ref, e2_ref, x_ref, o_ref,
                    acc_ref, ga_ref, a_ref, *, n1, heads):
    j = pl.program_id(1)
    nk = r2_ref.shape[1]
    tm = ht_ref.shape[1]
    te = n1 * nk
    n_sub = nk // SUB16
    build = functools.partial(_build_gates, r2_ref, e2_ref, n1=n1, heads=heads)
    slot, next_slot = j % 2, (j + 1) % 2

    def put_slot(which):
        def put(k, s, c, tile):
            ga_ref[which, k * nk + s * SUB16:k * nk + (s + 1) * SUB16, c * GATE_LANES:(c + 1) * GATE_LANES] = tile
        return put

    @pl.when(j == 0)
    def _():
        acc_ref[...] = jnp.zeros_like(acc_ref)
        build(cnt_ref, c1_ref, 0, put_slot(0))

    act0 = jnp.dot(u_ref[0:te, :], ht_ref[...], preferred_element_type=F32)
    a_ref[0:te, :] = _gelu(act0).astype(BF16) * ga_ref[slot]
    build(cntn_ref, c1n_ref, 0, put_slot(next_slot))
    tiles = {}
    build(cnt_ref, c1_ref, n1, lambda k, s, c, tile: tiles.__setitem__((k, s, c), tile))
    gb = jnp.concatenate(
        [jnp.concatenate([tiles[k, s, c] for c in range(tm // GATE_LANES)], axis=1)
         for k in range(n1) for s in range(n_sub)], axis=0)
    act1 = jnp.dot(u_ref[te:2 * te, :], ht_ref[...], preferred_element_type=F32)
    a_ref[te:2 * te, :] = _gelu(act1).astype(BF16) * gb
    acc_ref[...] += jnp.dot(vt_ref[...], a_ref[...], preferred_element_type=F32)

    @pl.when(j == pl.num_programs(1) - 1)
    def _():
        o_ref[...] = x_ref[...] + acc_ref[...].T


def _experts(ht, u, vt, layer, cnt, c1, r2, e2, x, *, tm=512, n1=4):
    d, t = ht.shape
    heads, nk, _ = cnt.shape
    n_e = u.shape[1]
    tm = min(tm, t)
    te = n1 * nk
    n_j = n_e // (2 * te)
    assert n_e == n_j * 2 * te and (2 * n1) % 8 == 0
    rows_now = pl.BlockSpec((heads, 2 * n1, tm), lambda i, j: (0, j, i))
    rows_next = pl.BlockSpec((heads, 2 * n1, tm), lambda i, j: (0, jnp.minimum(j + 1, n_j - 1), i))
    tile_spec = pl.BlockSpec((heads, nk, tm), lambda i, j: (0, 0, i))
    return pl.pallas_call(
        functools.partial(_experts_kernel, n1=n1, heads=heads),
        grid=(t // tm, n_j),
        in_specs=[
            pl.BlockSpec((d, tm), lambda i, j: (0, i)),
            pl.BlockSpec((None, 2 * te, d), lambda i, j: (layer, j, 0)),
            pl.BlockSpec((None, d, 2 * te), lambda i, j: (layer, 0, j)),
            rows_now, rows_now, rows_next, rows_next, tile_spec, tile_spec,
            pl.BlockSpec((tm, d), lambda i, j: (i, 0)),
        ],
        out_specs=pl.BlockSpec((tm, d), lambda i, j: (i, 0)),
        out_shape=jax.ShapeDtypeStruct((t, d), F32),
        scratch_shapes=[
            pltpu.VMEM((d, tm), F32),
            pltpu.VMEM((2, te, tm), BF16),
            pltpu.VMEM((2 * te, tm), BF16),
        ],
        compiler_params=_params(("parallel", "arbitrary"), 56),
        name="peer_experts",
    )(ht, u, vt, cnt, c1, cnt, c1, r2, e2, x)


def _swap_halves(w):
    half = w.shape[-1] // 2
    return jnp.concatenate([w[..., half:], w[..., :half]], axis=-1)


def _prep_w_in(w_in, q_lora, kv_lora, nat_width):
    o1 = q_lora
    o2 = o1 + kv_lora
    o3 = o2 + MLA_ROPE
    w_in = w_in.astype(BF16)
    c_q, c_kv, k_pe, nat = w_in[..., :o1], w_in[..., o1:o2], w_in[..., o2:o3], w_in[..., o3:]
    assert nat.shape[-1] == 3 * nat_width
    k_sw = _swap_halves(k_pe)
    return jnp.concatenate([nat, c_q, c_kv, k_pe, k_pe, k_sw, k_sw], axis=-1)


def _prep_w_uq(w_uq):
    n_l, k, _ = w_uq.shape
    w = w_uq.reshape(n_l, k, MLA_HEADS, MLA_NOPE + MLA_ROPE)
    nope = w[..., :MLA_NOPE].reshape(n_l, k, MLA_HEADS * MLA_NOPE)
    pe = w[..., MLA_NOPE:]
    return jnp.concatenate([nope, pe.reshape(n_l, k, -1), _swap_halves(pe).reshape(n_l, k, -1)], axis=-1).astype(BF16)


def _prep_w_ukv(w_ukv):
    n_l, k, n = w_ukv.shape
    w = w_ukv.reshape(n_l, k, MLA_HEADS, n // MLA_HEADS)
    return jnp.concatenate([w[..., :MLA_NOPE].reshape(n_l, k, -1), w[..., MLA_NOPE:].reshape(n_l, k, -1)],
                           axis=-1).astype(BF16)


def _rope_tables(seq):
    inv = ROPE_BASE ** (-jnp.arange(0, MLA_ROPE, 2, dtype=F32) / MLA_ROPE)
    ang = jnp.arange(seq, dtype=F32)[:, None] * inv[None, :]
    cos, sin = jnp.cos(ang), jnp.sin(ang)
    return jnp.concatenate([cos, cos, cos, cos], axis=-1), jnp.concatenate([-sin, sin, -sin, sin], axis=-1)


def kernel(x, attn_norm, w_in, mla_q_norm, mla_w_uq, mla_kv_norm, mla_w_ukv, nat_rpb, mla_out_norm, nat_out_norm,
           w_out, ffn_norm, peer_w_q, peer_sub_keys, peer_u, peer_v, final_norm):
    batch, seq, d = x.shape
    depth = w_in.shape[0]
    q_lora, kv_lora = mla_q_norm.shape[1], mla_kv_norm.shape[1]
    nat_width = nat_rpb.shape[1] * NAT_HEAD_DIM
    mla_width = mla_out_norm.shape[1]
    assert mla_w_ukv.shape[2] == MLA_HEADS * 2 * LANES and mla_width == MLA_HEADS * LANES
    assert nat_width == mla_width and q_lora % LANES == 0 and kv_lora % LANES == 0
    t = batch * seq

    w1 = _prep_w_in(w_in, q_lora, kv_lora, nat_width)
    cq_col = 3 * nat_width // q_lora
    ckv_col = (3 * nat_width + q_lora) // kv_lora
    kpe_col = (3 * nat_width + q_lora + kv_lora) // LANES
    wq = _prep_w_uq(mla_w_uq)
    wkv = _prep_w_ukv(mla_w_ukv)
    wo = w_out.astype(BF16)
    wpq_t = jnp.swapaxes(peer_w_q, 1, 2).astype(BF16)
    keys = peer_sub_keys.astype(BF16)
    u = peer_u.astype(BF16)
    vt = jnp.swapaxes(peer_v, 1, 2).astype(BF16)
    bias_tbl = _nat_bias_tables(nat_rpb)
    cc, ss = _rope_tables(seq)

    xf = x.reshape(t, d)
    for l in range(depth):
        proj = _matmul([(xf, 0, d)], w1, l, out_dtype=BF16, gain=attn_norm[l], tn=2048, name="proj_in")
        q = _matmul([(proj, cq_col, q_lora)], wq, l, out_dtype=BF16, gain=mla_q_norm[l], tn=2048, name="mla_q_up")
        kv = _matmul([(proj, ckv_col, kv_lora)], wkv, l, out_dtype=BF16, gain=mla_kv_norm[l], tn=2048,
                     name="mla_kv_up")
        mla_o = _mla(q, kv, proj, cc, ss, mla_out_norm[l], batch=batch, seq=seq, kpe_col=kpe_col)
        nat_o = _nat(proj, bias_tbl, l, nat_out_norm[l], batch=batch, seq=seq)
        xf = _matmul([(mla_o, 0, mla_width), (nat_o, 0, nat_width)], wo, l, out_dtype=F32, res=xf, tn=2048,
                     name="mix_out")
        ht = _rmsnorm(xf, ffn_norm[l], out_dtype=BF16, transpose=True)
        cnt, c1, r2, e2 = _route(ht, wpq_t, keys, l)
        xf = _experts(ht, u, vt, l, cnt, c1, r2, e2, xf)
    out = _rmsnorm(xf, final_norm, out_dtype=F32)
    return out.reshape(batch, seq, d)
```

```python
import functools

import numpy as np
import jax
import jax.numpy as jnp
from jax import lax
from jax.experimental import pallas as pl
from jax.experimental.pallas import tpu as pltpu

F32 = jnp.float32
BF16 = jnp.bfloat16

RMS_EPS = 1e-6
GRID_W = 64
MLA_HEADS = 8
MLA_NOPE = 128
MLA_ROPE = 64
ROPE_BASE = 10000.0
NAT_HEAD_DIM = 64
NAT_WIN_R = 8
NAT_WIN_C = 16
PEER_HEADS = 8
PEER_TOPK = 16
LANES = 128
SUB16 = 16
NEG_BIG = -1e30
MIB = 1024 * 1024


def _params(sem, vmem_mib):
    return pltpu.CompilerParams(dimension_semantics=sem, vmem_limit_bytes=vmem_mib * MIB)


def _rms(x, g):
    ms = jnp.mean(x * x, axis=-1, keepdims=True)
    return x * lax.rsqrt(ms + RMS_EPS) * g


def _rmsnorm_kernel(x_ref, g_ref, o_ref, *, transpose):
    y = _rms(x_ref[...].astype(F32), g_ref[...])
    if transpose:
        y = y.T
    o_ref[...] = y.astype(o_ref.dtype)


def _rmsnorm(x, g, *, out_dtype, transpose=False, tm=512):
    t, d = x.shape
    tm = min(tm, t)
    if transpose:
        out_shape = jax.ShapeDtypeStruct((d, t), out_dtype)
        out_spec = pl.BlockSpec((d, tm), lambda i: (0, i))
    else:
        out_shape = jax.ShapeDtypeStruct((t, d), out_dtype)
        out_spec = pl.BlockSpec((tm, d), lambda i: (i, 0))
    return pl.pallas_call(
        functools.partial(_rmsnorm_kernel, transpose=transpose),
        grid=(t // tm,),
        in_specs=[pl.BlockSpec((tm, d), lambda i: (i, 0)), pl.BlockSpec((1, d), lambda i: (0, 0))],
        out_specs=out_spec,
        out_shape=out_shape,
        compiler_params=_params(("parallel",), 40),
        name="rmsnorm_t" if transpose else "rmsnorm",
    )(x, g.reshape(1, d).astype(F32))


def _mm_kernel(*refs, n_a, norm, has_res, norm_out):
    a_refs = refs[:n_a]
    pos = n_a
    g_ref = refs[pos] if norm else None
    pos += int(norm)
    w_ref = refs[pos]
    pos += 1
    res_ref = refs[pos] if has_res else None
    pos += int(has_res)
    g2_ref = refs[pos] if norm_out else None
    pos += int(norm_out)
    o_ref = refs[pos]
    pos += 1
    ot_ref = refs[pos] if norm_out else None
    pos += int(norm_out)
    an_ref = refs[pos]

    @pl.when(pl.program_id(1) == 0)
    def _():
        parts = [r[...] for r in a_refs]
        a = parts[0] if n_a == 1 else jnp.concatenate(parts, axis=1)
        if norm:
            a = _rms(a.astype(F32), g_ref[...])
        an_ref[...] = a.astype(BF16)

    acc = jnp.dot(an_ref[...], w_ref[...], preferred_element_type=F32)
    if has_res:
        acc = acc + res_ref[...]
    o_ref[...] = acc.astype(o_ref.dtype)
    if norm_out:
        ot_ref[...] = _rms(acc, g2_ref[...]).T.astype(ot_ref.dtype)


def _matmul(a_list, w, layer, *, out_dtype, gain=None, res=None, gain_out=None, tm=512, tn=512, name="matmul"):
    t = a_list[0][0].shape[0]
    _, k, n = w.shape
    assert k == sum(width for _, _, width in a_list)
    tm, tn = min(tm, t), min(tn, n)
    in_specs, args = [], []
    for arr, cb, width in a_list:
        in_specs.append(pl.BlockSpec((tm, width), lambda i, j, cb=cb: (i, cb)))
        args.append(arr)
    if gain is not None:
        in_specs.append(pl.BlockSpec((1, k), lambda i, j: (0, 0)))
        args.append(gain.reshape(1, k).astype(F32))
    in_specs.append(pl.BlockSpec((None, k, tn), lambda i, j: (layer, 0, j)))
    args.append(w)
    if res is not None:
        in_specs.append(pl.BlockSpec((tm, tn), lambda i, j: (i, j)))
        args.append(res)
    out_specs = pl.BlockSpec((tm, tn), lambda i, j: (i, j))
    out_shape = jax.ShapeDtypeStruct((t, n), out_dtype)
    if gain_out is not None:
        assert tn == n
        in_specs.append(pl.BlockSpec((1, n), lambda i, j: (0, 0)))
        args.append(gain_out.reshape(1, n).astype(F32))
        out_specs = [out_specs, pl.BlockSpec((n, tm), lambda i, j: (0, i))]
        out_shape = [out_shape, jax.ShapeDtypeStruct((n, t), BF16)]
    return pl.pallas_call(
        functools.partial(_mm_kernel, n_a=len(a_list), norm=gain is not None, has_res=res is not None,
                          norm_out=gain_out is not None),
        grid=(t // tm, n // tn),
        in_specs=in_specs,
        out_specs=out_specs,
        out_shape=out_shape,
        scratch_shapes=[pltpu.VMEM((tm, k), BF16)],
        compiler_params=_params(("parallel", "arbitrary"), 48),
        name=name,
    )(*args)


def _mla_kernel(q_ref, k_ref, v_ref, ka_ref, kb_ref, cck_ref, ssk_ref, ccq_ref, ssq_ref, g_ref, o_ref, kcat_ref,
                s_ref, p_ref, *, heads, scale):
    @pl.when(pl.program_id(1) == 0)
    def _():
        rk = ka_ref[...].astype(F32) * cck_ref[...] + kb_ref[...].astype(F32) * ssk_ref[...]
        rk = rk.astype(BF16)
        for h in range(heads):
            kcat_ref[h, :, 0:LANES] = k_ref[:, h * LANES:(h + 1) * LANES]
            kcat_ref[h, :, LANES:2 * LANES] = rk

    tq = q_ref.shape[0]
    pe0 = heads * MLA_NOPE
    sw0 = pe0 + (heads // 2) * LANES
    ccq, ssq = ccq_ref[...], ssq_ref[...]
    lane = lax.broadcasted_iota(jnp.int32, (tq, LANES), 1)

    def scores(h):
        p_lo = pe0 + (h // 2) * LANES
        s_lo = sw0 + (h // 2) * LANES
        rq = q_ref[:, p_lo:p_lo + LANES].astype(F32) * ccq + q_ref[:, s_lo:s_lo + LANES].astype(F32) * ssq
        keep = (lane < MLA_ROPE) if h % 2 == 0 else (lane >= MLA_ROPE)
        rq = jnp.where(keep, rq, 0.0).astype(BF16)
        qcat = jnp.concatenate([q_ref[:, h * LANES:(h + 1) * LANES], rq], axis=1)
        s_ref[h % 2] = lax.dot_general(qcat, kcat_ref[h], (((1,), (1,)), ((), ())), preferred_element_type=F32)

    def softmax(h):
        c = scale * 1.4426950408889634
        inv = []
        for r in range(tq // SUB16):
            rows = slice(r * SUB16, (r + 1) * SUB16)
            s = s_ref[h % 2, rows, :]
            m = jnp.max(s, axis=-1, keepdims=True)
            p = jnp.exp2((s - m) * c)
            inv.append(1.0 / jnp.sum(p, axis=-1, keepdims=True))
            p_ref[h % 2, rows, :] = p.astype(BF16)
        return jnp.concatenate(inv, axis=0)

    def values(h, inv_l):
        return jnp.dot(p_ref[h % 2], v_ref[:, h * LANES:(h + 1) * LANES], preferred_element_type=F32) * inv_l

    outs, inv_l = [], {}
    scores(0)
    for h in range(heads):
        if h + 1 < heads:
            scores(h + 1)
        inv_l[h] = softmax(h)
        if h >= 1:
            outs.append(values(h - 1, inv_l[h - 1]))
    outs.append(values(heads - 1, inv_l[heads - 1]))
    o_all = jnp.concatenate(outs, axis=1)
    o_ref[...] = _rms(o_all, g_ref[...]).astype(o_ref.dtype)


def _mla(q, kv, proj, cc, ss, gain, *, batch, seq, kpe_col, tq=256):
    heads = MLA_HEADS
    t = q.shape[0]
    tq = min(tq, seq)
    nq = seq // tq
    width = heads * LANES
    return pl.pallas_call(
        functools.partial(_mla_kernel, heads=heads, scale=float((MLA_NOPE + MLA_ROPE) ** -0.5)),
        grid=(batch, nq),
        in_specs=[
            pl.BlockSpec((tq, q.shape[1]), lambda b, i: (b * nq + i, 0)),
            pl.BlockSpec((seq, width), lambda b, i: (b, 0)),
            pl.BlockSpec((seq, width), lambda b, i: (b, 1)),
            pl.BlockSpec((seq, LANES), lambda b, i: (b, kpe_col)),
            pl.BlockSpec((seq, LANES), lambda b, i: (b, kpe_col + 1)),
            pl.BlockSpec((seq, LANES), lambda b, i: (0, 0)),
            pl.BlockSpec((seq, LANES), lambda b, i: (0, 0)),
            pl.BlockSpec((tq, LANES), lambda b, i: (i, 0)),
            pl.BlockSpec((tq, LANES), lambda b, i: (i, 0)),
            pl.BlockSpec((1, width), lambda b, i: (0, 0)),
        ],
        out_specs=pl.BlockSpec((tq, width), lambda b, i: (b * nq + i, 0)),
        out_shape=jax.ShapeDtypeStruct((t, width), BF16),
        scratch_shapes=[
            pltpu.VMEM((heads, seq, 2 * LANES), BF16),
            pltpu.VMEM((2, tq, seq), F32),
            pltpu.VMEM((2, tq, seq), BF16),
        ],
        compiler_params=_params(("parallel", "arbitrary"), 56),
        name="mla_attention",
    )(q, kv, kv, proj, proj, cc, ss, cc, ss, gain.reshape(1, width).astype(F32))


def _nat_kernel(q_ref, k_ref, v_ref, bias_ref, g_ref, o_ref, s_ref, p_ref, *, rows, pairs):
    r = pl.program_id(1)
    rs = jnp.clip(r - NAT_WIN_R // 2, 0, rows - NAT_WIN_R)
    start = rs - r + (NAT_WIN_R - 1)
    koff = pl.multiple_of(rs * GRID_W, GRID_W)
    band = NAT_WIN_R * GRID_W
    lane = lax.broadcasted_iota(jnp.int32, (GRID_W, LANES), 1)
    first = lane < NAT_HEAD_DIM
    for j in range(pairs):
        cols = slice(j * LANES, (j + 1) * LANES)
        q2 = q_ref[:, cols] * jnp.asarray(NAT_HEAD_DIM ** -0.5, BF16)
        zero = jnp.zeros_like(q2)
        qq = jnp.concatenate([jnp.where(first, q2, zero), jnp.where(first, zero, q2)], axis=0)
        kb = k_ref[pl.ds(koff, band), cols]
        s_ref[j] = lax.dot_general(qq, kb, (((1,), (1,)), ((), ())), preferred_element_type=F32)
    inv_l = []
    for j in range(pairs):
        bias = jnp.concatenate([bias_ref[j, start + 2 * m] for m in range(NAT_WIN_R // 2)], axis=1)
        s = s_ref[j] + bias
        m = jnp.max(s, axis=-1, keepdims=True)
        p = jnp.exp(s - m)
        inv_l.append(1.0 / jnp.sum(p, axis=-1, keepdims=True))
        p_ref[j] = p.astype(BF16)
    outs = []
    for j in range(pairs):
        cols = slice(j * LANES, (j + 1) * LANES)
        vb = v_ref[pl.ds(koff, band), cols]
        o = jnp.dot(p_ref[j], vb, preferred_element_type=F32) * inv_l[j]
        outs.append(jnp.where(first, o[:GRID_W], o[GRID_W:]))
    o_all = jnp.concatenate(outs, axis=1)
    o_ref[...] = _rms(o_all, g_ref[...]).astype(o_ref.dtype)


def _nat(proj, bias_tbl, layer, gain, *, batch, seq):
    rows = seq // GRID_W
    assert rows >= NAT_WIN_R
    t = proj.shape[0]
    pairs, n_off = bias_tbl.shape[1], bias_tbl.shape[2]
    width = pairs * LANES
    return pl.pallas_call(
        functools.partial(_nat_kernel, rows=rows, pairs=pairs),
        grid=(batch, rows),
        in_specs=[
            pl.BlockSpec((GRID_W, width), lambda b, r: (b * rows + r, 0)),
            pl.BlockSpec((seq, width), lambda b, r: (b, 1)),
            pl.BlockSpec((seq, width), lambda b, r: (b, 2)),
            pl.BlockSpec((None, pairs, n_off, 2 * GRID_W, 2 * GRID_W), lambda b, r: (layer, 0, 0, 0, 0)),
            pl.BlockSpec((1, width), lambda b, r: (0, 0)),
        ],
        out_specs=pl.BlockSpec((GRID_W, width), lambda b, r: (b * rows + r, 0)),
        out_shape=jax.ShapeDtypeStruct((t, width), BF16),
        scratch_shapes=[
            pltpu.VMEM((pairs, 2 * GRID_W, NAT_WIN_R * GRID_W), F32),
            pltpu.VMEM((pairs, 2 * GRID_W, NAT_WIN_R * GRID_W), BF16),
        ],
        compiler_params=_params(("parallel", "arbitrary"), 48),
        name="nat_attention",
    )(proj, proj, proj, bias_tbl, gain.reshape(1, width).astype(F32))


def _nat_bias_tables(rpb):
    n_l, heads, n_off = rpb.shape[0], rpb.shape[1], rpb.shape[2]
    qc = np.arange(GRID_W)[:, None]
    kc = np.arange(GRID_W)[None, :]
    c_start = np.clip(qc - NAT_WIN_C // 2, 0, GRID_W - NAT_WIN_C)
    valid = (kc >= c_start) & (kc < c_start + NAT_WIN_C)
    dc_idx = np.clip(kc - qc, -(NAT_WIN_C - 1), NAT_WIN_C - 1) + (NAT_WIN_C - 1)
    tbl = rpb.astype(F32)[:, :, :, dc_idx]
    tbl = jnp.where(jnp.asarray(valid), tbl, NEG_BIG)
    tbl = tbl.reshape(n_l, heads // 2, 2, n_off, GRID_W, GRID_W)
    tbl = jnp.transpose(tbl, (0, 1, 3, 2, 4, 5)).reshape(n_l, heads // 2, n_off, 2 * GRID_W, GRID_W)
    return jnp.concatenate([tbl[:, :, :-1], tbl[:, :, 1:]], axis=-1)


def _extract(x, rows, exact):
    m = jnp.max(x, axis=0, keepdims=True)
    hit = x == m
    if exact:
        first = jnp.min(jnp.where(hit, rows, float(x.shape[0])), axis=0, keepdims=True)
        hit = rows == first
    return m, hit


def _row_index(shape):
    return lax.broadcasted_iota(jnp.int32, shape, 0).astype(F32)


def _top_rows_ranked(x, dst_ref, k, exact):
    rows = _row_index(x.shape) if exact else None
    rank = jnp.full(x.shape, float(k), F32)
    for i in range(k):
        m, hit = _extract(x, rows, exact)
        dst_ref[i:i + 1, :] = m
        rank = jnp.where(hit, float(i), rank)
        x = jnp.where(hit, -jnp.inf, x)
    return rank, x


def _top2_rows(x1, x2, dst1_ref, dst2_ref, k, exact):
    rows = _row_index(x1.shape) if exact else None
    rank1 = jnp.full(x1.shape, float(k), F32) if exact else None
    rank2 = jnp.full(x2.shape, float(k), F32)
    for i in range(k):
        m1, hit1 = _extract(x1, rows, exact)
        m2, hit2 = _extract(x2, rows, exact)
        dst1_ref[i:i + 1, :] = m1
        dst2_ref[i:i + 1, :] = m2
        if exact:
            rank1 = jnp.where(hit1, float(i), rank1)
        x1 = jnp.where(hit1, -jnp.inf, x1)
        rank2 = jnp.where(hit2, float(i), rank2)
        x2 = jnp.where(hit2, -jnp.inf, x2)
    return rank1, rank2, x1, x2


def _count_removed(x):
    return jnp.sum(jnp.where(x == -jnp.inf, 1.0, 0.0), axis=0, keepdims=True)


def _dup_bf16_bits(x):
    bits = pltpu.bitcast(x.astype(BF16).astype(F32), jnp.uint32)
    return bits | (bits >> 16)


def _route_group(s1, s2, a_ref, b_ref, cand_ref, top_ref, exact):
    nk = s1.shape[0]
    rank1, rank2, x1, x2 = _top2_rows(s1, s2, a_ref, b_ref, PEER_TOPK, exact)
    offs, off = [], 0
    for r in range(PEER_TOPK):
        n_r = PEER_TOPK // (r + 1)
        cand_ref[off:off + n_r, :] = a_ref[r:r + 1, :] + b_ref[0:n_r, :]
        offs.append((off, n_r))
        off += n_r
    n_pad = cand_ref.shape[0] - off
    if n_pad:
        cand_ref[off:, :] = jnp.full((n_pad, LANES), -jnp.inf, F32)
    crank, xc = _top_rows_ranked(cand_ref[...], top_ref, PEER_TOPK, exact)
    taken = jnp.where(crank < float(PEER_TOPK), 1.0, 0.0)
    top = top_ref[...]
    z = jnp.sum(jnp.exp(top - top[0:1]), axis=0, keepdims=True)
    cnt = jnp.zeros((nk, LANES), F32)
    for r, (off, n_r) in enumerate(offs):
        cnt_r = jnp.sum(taken[off:off + n_r], axis=0, keepdims=True)
        is_r = (rank1 == float(r)) if exact else (s1 == a_ref[r:r + 1, :])
        cnt = jnp.where(is_r, cnt_r, cnt)
    c1 = jnp.exp(s1 - a_ref[0:1, :]) / z
    e2 = jnp.exp(s2 - b_ref[0:1, :])
    ties = None
    if not exact:
        k = float(PEER_TOPK)
        ties = ((_count_removed(x1) > k) | (_count_removed(x2) > k) | (_count_removed(xc) > k + n_pad))
    return cnt, c1, rank2, e2, ties


def _route_kernel(ht_ref, wq_ref, keys_ref, cnt_ref, c1_ref, r2_ref, e2_ref, a_ref, b_ref, cand_ref, top_ref):
    tm = ht_ref.shape[1]
    qt = jnp.dot(wq_ref[...], ht_ref[...], preferred_element_type=F32)
    half = qt.shape[0] // 2
    s1_all = jnp.dot(keys_ref[0], qt[:half].astype(BF16), preferred_element_type=F32)
    s2_all = jnp.dot(keys_ref[1], qt[half:].astype(BF16), preferred_element_type=F32)

    def store(c, cnt, c1, rank2, e2):
        lanes = slice(c * LANES, (c + 1) * LANES)
        cnt_ref[0, :, lanes] = _dup_bf16_bits(cnt)
        c1_ref[0, :, lanes] = _dup_bf16_bits(c1)
        r2_ref[0, :, lanes] = rank2.astype(BF16)
        e2_ref[0, :, lanes] = e2.astype(BF16)

    def group(c, exact):
        lanes = slice(c * LANES, (c + 1) * LANES)
        return _route_group(s1_all[:, lanes], s2_all[:, lanes], a_ref, b_ref, cand_ref, top_ref, exact)

    any_tie = jnp.zeros((1, LANES), F32)
    for c in range(tm // LANES):
        cnt, c1, rank2, e2, ties = group(c, exact=False)
        store(c, cnt, c1, rank2, e2)
        any_tie = jnp.maximum(any_tie, jnp.where(ties, 1.0, 0.0))

    @pl.when(jnp.max(any_tie) > 0.0)
    def _():
        for c in range(tm // LANES):
            store(c, *group(c, exact=True)[:4])


def _route(ht, wqt, keys, layer, *, tm=1024):
    d, t = ht.shape
    heads = PEER_HEADS
    dq2 = wqt.shape[1] // heads
    nk = keys.shape[2]
    tm = min(tm, t)
    n_cand = -(-sum(PEER_TOPK // (r + 1) for r in range(PEER_TOPK)) // 8) * 8
    row_shape = jax.ShapeDtypeStruct((heads, nk, t), jnp.uint32)
    row_spec = pl.BlockSpec((1, nk, tm), lambda i, h: (h, 0, i))
    tile_shape = jax.ShapeDtypeStruct((heads, nk, t), BF16)
    tile_spec = pl.BlockSpec((1, nk, tm), lambda i, h: (h, 0, i))
    return pl.pallas_call(
        _route_kernel,
        grid=(t // tm, heads),
        in_specs=[
            pl.BlockSpec((d, tm), lambda i, h: (0, i)),
            pl.BlockSpec((None, dq2, d), lambda i, h: (layer, h, 0)),
            pl.BlockSpec((None,) + keys.shape[1:], lambda i, h: (layer, 0, 0, 0)),
        ],
        out_specs=[row_spec, row_spec, tile_spec, tile_spec],
        out_shape=[row_shape, row_shape, tile_shape, tile_shape],
        scratch_shapes=[
            pltpu.VMEM((PEER_TOPK, LANES), F32),
            pltpu.VMEM((PEER_TOPK, LANES), F32),
            pltpu.VMEM((n_cand, LANES), F32),
            pltpu.VMEM((PEER_TOPK, LANES), F32),
        ],
        compiler_params=_params(("parallel", "arbitrary"), 40),
        name="peer_route",
    )(ht, wqt, keys)


def _gelu(x):
    return 0.5 * x * (1.0 + lax.erf(x * 0.7071067811865476))


GATE_LANES = 256
GATE_KGROUP = 2


def _row_as_bf16(row):
    return pltpu.bitcast(jnp.broadcast_to(row, (8, row.shape[1])), BF16)


def _build_gates(r2_ref, e2_ref, cnt_ref, c1_ref, row0, put, *, n1, heads):
    nk = r2_ref.shape[1]
    tm = r2_ref.shape[2]
    n_sub = nk // SUB16
    for k0 in range(0, n1, GATE_KGROUP):
        ks = range(k0, min(k0 + GATE_KGROUP, n1))
        for c in range(tm // GATE_LANES):
            lanes = slice(c * GATE_LANES, (c + 1) * GATE_LANES)
            g = {(k, s): jnp.zeros((SUB16, GATE_LANES), BF16) for k in ks for s in range(n_sub)}
            for h in range(heads):
                cnt = {k: _row_as_bf16(cnt_ref[h, row0 + k:row0 + k + 1, lanes]) for k in ks}
                c1 = {k: _row_as_bf16(c1_ref[h, row0 + k:row0 + k + 1, lanes]) for k in ks}
                for s in range(n_sub):
                    rows = slice(s * SUB16, (s + 1) * SUB16)
                    r2 = r2_ref[h, rows, lanes]
                    e2 = e2_ref[h, rows, lanes]
                    for k in ks:
                        g[k, s] = g[k, s] + jnp.where(r2 < cnt[k], e2, jnp.zeros_like(e2)) * c1[k]
            for k in ks:
                for s in range(n_sub):
                    put(k, s, c, g[k, s])


def _experts_kernel(ht_ref, u_ref, vt_ref, cnt_ref, c1_ref, cntn_ref, c1n_ref, r2_ref, e2_ref, x_ref, gf_ref,
                    o_ref, acc_ref, ga_ref, a_ref, *, n1, heads, final_norm):
    j = pl.program_id(1)
    nk = r2_ref.shape[1]
    tm = ht_ref.shape[1]
    te = n1 * nk
    n_sub = nk // SUB16
    build = functools.partial(_build_gates, r2_ref, e2_ref, n1=n1, heads=heads)
    slot, next_slot = j % 2, (j + 1) % 2

    def put_slot(which):
        def put(k, s, c, tile):
            ga_ref[which, k * nk + s * SUB16:k * nk + (s + 1) * SUB16, c * GATE_LANES:(c + 1) * GATE_LANES] = tile
        return put

    @pl.when(j == 0)
    def _():
        acc_ref[...] = jnp.zeros_like(acc_ref)
        build(cnt_ref, c1_ref, 0, put_slot(0))

    act0 = jnp.dot(u_ref[0:te, :], ht_ref[...], preferred_element_type=F32)
    a_ref[0:te, :] = _gelu(act0).astype(BF16) * ga_ref[slot]
    build(cntn_ref, c1n_ref, 0, put_slot(next_slot))
    tiles = {}
    build(cnt_ref, c1_ref, n1, lambda k, s, c, tile: tiles.__setitem__((k, s, c), tile))
    gb = jnp.concatenate(
        [jnp.concatenate([tiles[k, s, c] for c in range(tm // GATE_LANES)], axis=1)
         for k in range(n1) for s in range(n_sub)], axis=0)
    act1 = jnp.dot(u_ref[te:2 * te, :], ht_ref[...], preferred_element_type=F32)
    a_ref[te:2 * te, :] = _gelu(act1).astype(BF16) * gb
    acc_ref[...] += jnp.dot(vt_ref[...], a_ref[...], preferred_element_type=F32)

    @pl.when(j == pl.num_programs(1) - 1)
    def _():
        y = x_ref[...] + acc_ref[...].T
        o_ref[...] = _rms(y, gf_ref[...]) if final_norm else y


def _experts(ht, u, vt, layer, cnt, c1, r2, e2, x, gain_final=None, *, tm=512, n1=4):
    d, t = ht.shape
    heads, nk, _ = cnt.shape
    n_e = u.shape[1]
    tm = min(tm, t)
    te = n1 * nk
    n_j = n_e // (2 * te)
    assert n_e == n_j * 2 * te and (2 * n1) % 8 == 0
    rows_now = pl.BlockSpec((heads, 2 * n1, tm), lambda i, j: (0, j, i))
    rows_next = pl.BlockSpec((heads, 2 * n1, tm), lambda i, j: (0, jnp.minimum(j + 1, n_j - 1), i))
    tile_spec = pl.BlockSpec((heads, nk, tm), lambda i, j: (0, 0, i))
    return pl.pallas_call(
        functools.partial(_experts_kernel, n1=n1, heads=heads, final_norm=gain_final is not None),
        grid=(t // tm, n_j),
        in_specs=[
            pl.BlockSpec((d, tm), lambda i, j: (0, i)),
            pl.BlockSpec((None, 2 * te, d), lambda i, j: (layer, j, 0)),
            pl.BlockSpec((None, d, 2 * te), lambda i, j: (layer, 0, j)),
            rows_now, rows_now, rows_next, rows_next, tile_spec, tile_spec,
            pl.BlockSpec((tm, d), lambda i, j: (i, 0)),
            pl.BlockSpec((1, d), lambda i, j: (0, 0)),
        ],
        out_specs=pl.BlockSpec((tm, d), lambda i, j: (i, 0)),
        out_shape=jax.ShapeDtypeStruct((t, d), F32),
        scratch_shapes=[
            pltpu.VMEM((d, tm), F32),
            pltpu.VMEM((2, te, tm), BF16),
            pltpu.VMEM((2 * te, tm), BF16),
        ],
        compiler_params=_params(("parallel", "arbitrary"), 56),
        name="peer_experts",
    )(ht, u, vt, cnt, c1, cnt, c1, r2, e2, x,
      (jnp.ones((d,), F32) if gain_final is None else gain_final).reshape(1, d).astype(F32))


def _swap_halves(w):
    half = w.shape[-1] // 2
    return jnp.concatenate([w[..., half:], w[..., :half]], axis=-1)


def _prep_w_in(w_in, q_lora, kv_lora, nat_width):
    o1 = q_lora
    o2 = o1 + kv_lora
    o3 = o2 + MLA_ROPE
    w_in = w_in.astype(BF16)
    c_q, c_kv, k_pe, nat = w_in[..., :o1], w_in[..., o1:o2], w_in[..., o2:o3], w_in[..., o3:]
    assert nat.shape[-1] == 3 * nat_width
    k_sw = _swap_halves(k_pe)
    return jnp.concatenate([nat, c_q, c_kv, k_pe, k_pe, k_sw, k_sw], axis=-1)


def _prep_w_uq(w_uq):
    n_l, k, _ = w_uq.shape
    w = w_uq.reshape(n_l, k, MLA_HEADS, MLA_NOPE + MLA_ROPE)
    nope = w[..., :MLA_NOPE].reshape(n_l, k, MLA_HEADS * MLA_NOPE)
    pe = w[..., MLA_NOPE:]
    return jnp.concatenate([nope, pe.reshape(n_l, k, -1), _swap_halves(pe).reshape(n_l, k, -1)], axis=-1).astype(BF16)


def _prep_w_ukv(w_ukv):
    n_l, k, n = w_ukv.shape
    w = w_ukv.reshape(n_l, k, MLA_HEADS, n // MLA_HEADS)
    return jnp.concatenate([w[..., :MLA_NOPE].reshape(n_l, k, -1), w[..., MLA_NOPE:].reshape(n_l, k, -1)],
                           axis=-1).astype(BF16)


def _rope_tables(seq):
    inv = ROPE_BASE ** (-jnp.arange(0, MLA_ROPE, 2, dtype=F32) / MLA_ROPE)
    ang = jnp.arange(seq, dtype=F32)[:, None] * inv[None, :]
    cos, sin = jnp.cos(ang), jnp.sin(ang)
    return jnp.concatenate([cos, cos, cos, cos], axis=-1), jnp.concatenate([-sin, sin, -sin, sin], axis=-1)


def kernel(x, attn_norm, w_in, mla_q_norm, mla_w_uq, mla_kv_norm, mla_w_ukv, nat_rpb, mla_out_norm, nat_out_norm,
           w_out, ffn_norm, peer_w_q, peer_sub_keys, peer_u, peer_v, final_norm):
    batch, seq, d = x.shape
    depth = w_in.shape[0]
    q_lora, kv_lora = mla_q_norm.shape[1], mla_kv_norm.shape[1]
    nat_width = nat_rpb.shape[1] * NAT_HEAD_DIM
    mla_width = mla_out_norm.shape[1]
    assert mla_w_ukv.shape[2] == MLA_HEADS * 2 * LANES and mla_width == MLA_HEADS * LANES
    assert nat_width == mla_width and q_lora % LANES == 0 and kv_lora % LANES == 0
    t = batch * seq

    w1 = _prep_w_in(w_in, q_lora, kv_lora, nat_width)
    cq_col = 3 * nat_width // q_lora
    ckv_col = (3 * nat_width + q_lora) // kv_lora
    kpe_col = (3 * nat_width + q_lora + kv_lora) // LANES
    wq = _prep_w_uq(mla_w_uq)
    wkv = _prep_w_ukv(mla_w_ukv)
    wo = w_out.astype(BF16)
    wpq_t = jnp.swapaxes(peer_w_q, 1, 2).astype(BF16)
    keys = peer_sub_keys.astype(BF16)
    u = peer_u.astype(BF16)
    vt = jnp.swapaxes(peer_v, 1, 2).astype(BF16)
    bias_tbl = _nat_bias_tables(nat_rpb)
    cc, ss = _rope_tables(seq)

    xf = x.reshape(t, d)
    for l in range(depth):
        proj = _matmul([(xf, 0, d)], w1, l, out_dtype=BF16, gain=attn_norm[l], tn=2048, name="proj_in")
        q = _matmul([(proj, cq_col, q_lora)], wq, l, out_dtype=BF16, gain=mla_q_norm[l], tn=2048, name="mla_q_up")
        kv = _matmul([(proj, ckv_col, kv_lora)], wkv, l, out_dtype=BF16, gain=mla_kv_norm[l], tn=2048,
                     name="mla_kv_up")
        mla_o = _mla(q, kv, proj, cc, ss, mla_out_norm[l], batch=batch, seq=seq, kpe_col=kpe_col)
        nat_o = _nat(proj, bias_tbl, l, nat_out_norm[l], batch=batch, seq=seq)
        xf, ht = _matmul([(mla_o, 0, mla_width), (nat_o, 0, nat_width)], wo, l, out_dtype=F32, res=xf,
                         gain_out=ffn_norm[l], tn=d, name="mix_out")
        cnt, c1, r2, e2 = _route(ht, wpq_t, keys, l)
        xf = _experts(ht, u, vt, l, cnt, c1, r2, e2, xf, final_norm if l == depth - 1 else None)
    return xf.reshape(batch, seq, d)
```

```python
import functools

import numpy as np
import jax
import jax.numpy as jnp
from jax import lax
from jax.experimental import pallas as pl
from jax.experimental.pallas import tpu as pltpu

F32 = jnp.float32
BF16 = jnp.bfloat16

RMS_EPS = 1e-6
GRID_W = 64
MLA_HEADS = 8
MLA_NOPE = 128
MLA_ROPE = 64
ROPE_BASE = 10000.0
NAT_HEAD_DIM = 64
NAT_WIN_R = 8
NAT_WIN_C = 16
PEER_HEADS = 8
PEER_TOPK = 16
LANES = 128
SUB16 = 16
NEG_BIG = -1e30
MIB = 1024 * 1024


def _params(sem, vmem_mib):
    return pltpu.CompilerParams(dimension_semantics=sem, vmem_limit_bytes=vmem_mib * MIB)


def _rms(x, g):
    ms = jnp.mean(x * x, axis=-1, keepdims=True)
    return x * lax.rsqrt(ms + RMS_EPS) * g


def _rmsnorm_kernel(x_ref, g_ref, o_ref, *, transpose):
    y = _rms(x_ref[...].astype(F32), g_ref[...])
    if transpose:
        y = y.T
    o_ref[...] = y.astype(o_ref.dtype)


def _rmsnorm(x, g, *, out_dtype, transpose=False, tm=512):
    t, d = x.shape
    tm = min(tm, t)
    if transpose:
        out_shape = jax.ShapeDtypeStruct((d, t), out_dtype)
        out_spec = pl.BlockSpec((d, tm), lambda i: (0, i))
    else:
        out_shape = jax.ShapeDtypeStruct((t, d), out_dtype)
        out_spec = pl.BlockSpec((tm, d), lambda i: (i, 0))
    return pl.pallas_call(
        functools.partial(_rmsnorm_kernel, transpose=transpose),
        grid=(t // tm,),
        in_specs=[pl.BlockSpec((tm, d), lambda i: (i, 0)), pl.BlockSpec((1, d), lambda i: (0, 0))],
        out_specs=out_spec,
        out_shape=out_shape,
        compiler_params=_params(("parallel",), 40),
        name="rmsnorm_t" if transpose else "rmsnorm",
    )(x, g.reshape(1, d).astype(F32))


def _mm_kernel(*refs, n_a, norm, has_res, norm_out):
    a_refs = refs[:n_a]
    pos = n_a
    g_ref = refs[pos] if norm else None
    pos += int(norm)
    w_ref = refs[pos]
    pos += 1
    res_ref = refs[pos] if has_res else None
    pos += int(has_res)
    g2_ref = refs[pos] if norm_out else None
    pos += int(norm_out)
    o_ref = refs[pos]
    pos += 1
    ot_ref = refs[pos] if norm_out else None
    pos += int(norm_out)
    an_ref = refs[pos]

    @pl.when(pl.program_id(1) == 0)
    def _():
        parts = [r[...] for r in a_refs]
        a = parts[0] if n_a == 1 else jnp.concatenate(parts, axis=1)
        if norm:
            a = _rms(a.astype(F32), g_ref[...])
        an_ref[...] = a.astype(BF16)

    acc = jnp.dot(an_ref[...], w_ref[...], preferred_element_type=F32)
    if has_res:
        acc = acc + res_ref[...]
    o_ref[...] = acc.astype(o_ref.dtype)
    if norm_out:
        ot_ref[...] = _rms(acc, g2_ref[...]).T.astype(ot_ref.dtype)


def _matmul(a_list, w, layer, *, out_dtype, gain=None, res=None, gain_out=None, tm=512, tn=512, name="matmul"):
    t = a_list[0][0].shape[0]
    _, k, n = w.shape
    assert k == sum(width for _, _, width in a_list)
    tm, tn = min(tm, t), min(tn, n)
    in_specs, args = [], []
    for arr, cb, width in a_list:
        in_specs.append(pl.BlockSpec((tm, width), lambda i, j, cb=cb: (i, cb)))
        args.append(arr)
    if gain is not None:
        in_specs.append(pl.BlockSpec((1, k), lambda i, j: (0, 0)))
        args.append(gain.reshape(1, k).astype(F32))
    in_specs.append(pl.BlockSpec((None, k, tn), lambda i, j: (layer, 0, j)))
    args.append(w)
    if res is not None:
        in_specs.append(pl.BlockSpec((tm, tn), lambda i, j: (i, j)))
        args.append(res)
    out_specs = pl.BlockSpec((tm, tn), lambda i, j: (i, j))
    out_shape = jax.ShapeDtypeStruct((t, n), out_dtype)
    if gain_out is not None:
        assert tn == n
        in_specs.append(pl.BlockSpec((1, n), lambda i, j: (0, 0)))
        args.append(gain_out.reshape(1, n).astype(F32))
        out_specs = [out_specs, pl.BlockSpec((n, tm), lambda i, j: (0, i))]
        out_shape = [out_shape, jax.ShapeDtypeStruct((n, t), BF16)]
    return pl.pallas_call(
        functools.partial(_mm_kernel, n_a=len(a_list), norm=gain is not None, has_res=res is not None,
                          norm_out=gain_out is not None),
        grid=(t // tm, n // tn),
        in_specs=in_specs,
        out_specs=out_specs,
        out_shape=out_shape,
        scratch_shapes=[pltpu.VMEM((tm, k), BF16)],
        compiler_params=_params(("parallel", "arbitrary"), 48),
        name=name,
    )(*args)


def _mla_kernel(q_ref, k_ref, v_ref, ka_ref, kb_ref, cck_ref, ssk_ref, ccq_ref, ssq_ref, g_ref, o_ref, kcat_ref,
                s_ref, p_ref, *, heads, scale):
    @pl.when(pl.program_id(1) == 0)
    def _():
        rk = ka_ref[...].astype(F32) * cck_ref[...] + kb_ref[...].astype(F32) * ssk_ref[...]
        rk = rk.astype(BF16)
        for h in range(heads):
            kcat_ref[h, :, 0:LANES] = k_ref[:, h * LANES:(h + 1) * LANES]
            kcat_ref[h, :, LANES:2 * LANES] = rk

    tq = q_ref.shape[0]
    pe0 = heads * MLA_NOPE
    sw0 = pe0 + (heads // 2) * LANES
    ccq, ssq = ccq_ref[...], ssq_ref[...]
    lane = lax.broadcasted_iota(jnp.int32, (tq, LANES), 1)

    def scores(h):
        p_lo = pe0 + (h // 2) * LANES
        s_lo = sw0 + (h // 2) * LANES
        rq = q_ref[:, p_lo:p_lo + LANES].astype(F32) * ccq + q_ref[:, s_lo:s_lo + LANES].astype(F32) * ssq
        keep = (lane < MLA_ROPE) if h % 2 == 0 else (lane >= MLA_ROPE)
        rq = jnp.where(keep, rq, 0.0).astype(BF16)
        qcat = jnp.concatenate([q_ref[:, h * LANES:(h + 1) * LANES], rq], axis=1)
        s_ref[h % 2] = lax.dot_general(qcat, kcat_ref[h], (((1,), (1,)), ((), ())), preferred_element_type=F32)

    def softmax(h):
        c = scale * 1.4426950408889634
        inv = []
        for r in range(tq // SUB16):
            rows = slice(r * SUB16, (r + 1) * SUB16)
            s = s_ref[h % 2, rows, :]
            m = jnp.max(s, axis=-1, keepdims=True)
            p = jnp.exp2((s - m) * c)
            inv.append(1.0 / jnp.sum(p, axis=-1, keepdims=True))
            p_ref[h % 2, rows, :] = p.astype(BF16)
        return jnp.concatenate(inv, axis=0)

    def values(h, inv_l):
        return jnp.dot(p_ref[h % 2], v_ref[:, h * LANES:(h + 1) * LANES], preferred_element_type=F32) * inv_l

    outs, inv_l = [], {}
    scores(0)
    for h in range(heads):
        if h + 1 < heads:
            scores(h + 1)
        inv_l[h] = softmax(h)
        if h >= 1:
            outs.append(values(h - 1, inv_l[h - 1]))
    outs.append(values(heads - 1, inv_l[heads - 1]))
    o_all = jnp.concatenate(outs, axis=1)
    o_ref[...] = _rms(o_all, g_ref[...]).astype(o_ref.dtype)


def _mla(q, kv, proj, cc, ss, gain, *, batch, seq, kpe_col, tq=256):
    heads = MLA_HEADS
    t = q.shape[0]
    tq = min(tq, seq)
    nq = seq // tq
    width = heads * LANES
    return pl.pallas_call(
        functools.partial(_mla_kernel, heads=heads, scale=float((MLA_NOPE + MLA_ROPE) ** -0.5)),
        grid=(batch, nq),
        in_specs=[
            pl.BlockSpec((tq, q.shape[1]), lambda b, i: (b * nq + i, 0)),
            pl.BlockSpec((seq, width), lambda b, i: (b, 0)),
            pl.BlockSpec((seq, width), lambda b, i: (b, 1)),
            pl.BlockSpec((seq, LANES), lambda b, i: (b, kpe_col)),
            pl.BlockSpec((seq, LANES), lambda b, i: (b, kpe_col + 1)),
            pl.BlockSpec((seq, LANES), lambda b, i: (0, 0)),
            pl.BlockSpec((seq, LANES), lambda b, i: (0, 0)),
            pl.BlockSpec((tq, LANES), lambda b, i: (i, 0)),
            pl.BlockSpec((tq, LANES), lambda b, i: (i, 0)),
            pl.BlockSpec((1, width), lambda b, i: (0, 0)),
        ],
        out_specs=pl.BlockSpec((tq, width), lambda b, i: (b * nq + i, 0)),
        out_shape=jax.ShapeDtypeStruct((t, width), BF16),
        scratch_shapes=[
            pltpu.VMEM((heads, seq, 2 * LANES), BF16),
            pltpu.VMEM((2, tq, seq), F32),
            pltpu.VMEM((2, tq, seq), BF16),
        ],
        compiler_params=_params(("parallel", "arbitrary"), 56),
        name="mla_attention",
    )(q, kv, kv, proj, proj, cc, ss, cc, ss, gain.reshape(1, width).astype(F32))


def _nat_kernel(q_ref, k_ref, v_ref, bias_ref, g_ref, o_ref, s_ref, p_ref, *, rows, pairs):
    r = pl.program_id(1)
    rs = jnp.clip(r - NAT_WIN_R // 2, 0, rows - NAT_WIN_R)
    start = rs - r + (NAT_WIN_R - 1)
    koff = pl.multiple_of(rs * GRID_W, GRID_W)
    band = NAT_WIN_R * GRID_W
    lane = lax.broadcasted_iota(jnp.int32, (GRID_W, LANES), 1)
    first = lane < NAT_HEAD_DIM
    for j in range(pairs):
        cols = slice(j * LANES, (j + 1) * LANES)
        q2 = q_ref[:, cols] * jnp.asarray(NAT_HEAD_DIM ** -0.5, BF16)
        zero = jnp.zeros_like(q2)
        qq = jnp.concatenate([jnp.where(first, q2, zero), jnp.where(first, zero, q2)], axis=0)
        kb = k_ref[pl.ds(koff, band), cols]
        s_ref[j] = lax.dot_general(qq, kb, (((1,), (1,)), ((), ())), preferred_element_type=F32)
    inv_l = []
    for j in range(pairs):
        bias = jnp.concatenate([bias_ref[j, start + 2 * m] for m in range(NAT_WIN_R // 2)], axis=1)
        s = s_ref[j] + bias
        m = jnp.max(s, axis=-1, keepdims=True)
        p = jnp.exp(s - m)
        inv_l.append(1.0 / jnp.sum(p, axis=-1, keepdims=True))
        p_ref[j] = p.astype(BF16)
    outs = []
    for j in range(pairs):
        cols = slice(j * LANES, (j + 1) * LANES)
        vb = v_ref[pl.ds(koff, band), cols]
        o = jnp.dot(p_ref[j], vb, preferred_element_type=F32) * inv_l[j]
        outs.append(jnp.where(first, o[:GRID_W], o[GRID_W:]))
    o_all = jnp.concatenate(outs, axis=1)
    o_ref[...] = _rms(o_all, g_ref[...]).astype(o_ref.dtype)


def _nat(proj, bias_tbl, layer, gain, *, batch, seq):
    rows = seq // GRID_W
    assert rows >= NAT_WIN_R
    t = proj.shape[0]
    pairs, n_off = bias_tbl.shape[1], bias_tbl.shape[2]
    width = pairs * LANES
    return pl.pallas_call(
        functools.partial(_nat_kernel, rows=rows, pairs=pairs),
        grid=(batch, rows),
        in_specs=[
            pl.BlockSpec((GRID_W, width), lambda b, r: (b * rows + r, 0)),
            pl.BlockSpec((seq, width), lambda b, r: (b, 1)),
            pl.BlockSpec((seq, width), lambda b, r: (b, 2)),
            pl.BlockSpec((None, pairs, n_off, 2 * GRID_W, 2 * GRID_W), lambda b, r: (layer, 0, 0, 0, 0)),
            pl.BlockSpec((1, width), lambda b, r: (0, 0)),
        ],
        out_specs=pl.BlockSpec((GRID_W, width), lambda b, r: (b * rows + r, 0)),
        out_shape=jax.ShapeDtypeStruct((t, width), BF16),
        scratch_shapes=[
            pltpu.VMEM((pairs, 2 * GRID_W, NAT_WIN_R * GRID_W), F32),
            pltpu.VMEM((pairs, 2 * GRID_W, NAT_WIN_R * GRID_W), BF16),
        ],
        compiler_params=_params(("parallel", "arbitrary"), 48),
        name="nat_attention",
    )(proj, proj, proj, bias_tbl, gain.reshape(1, width).astype(F32))


def _nat_bias_tables(rpb):
    n_l, heads, n_off = rpb.shape[0], rpb.shape[1], rpb.shape[2]
    qc = np.arange(GRID_W)[:, None]
    kc = np.arange(GRID_W)[None, :]
    c_start = np.clip(qc - NAT_WIN_C // 2, 0, GRID_W - NAT_WIN_C)
    valid = (kc >= c_start) & (kc < c_start + NAT_WIN_C)
    dc_idx = np.clip(kc - qc, -(NAT_WIN_C - 1), NAT_WIN_C - 1) + (NAT_WIN_C - 1)
    tbl = rpb.astype(F32)[:, :, :, dc_idx]
    tbl = jnp.where(jnp.asarray(valid), tbl, NEG_BIG)
    tbl = tbl.reshape(n_l, heads // 2, 2, n_off, GRID_W, GRID_W)
    tbl = jnp.transpose(tbl, (0, 1, 3, 2, 4, 5)).reshape(n_l, heads // 2, n_off, 2 * GRID_W, GRID_W)
    return jnp.concatenate([tbl[:, :, :-1], tbl[:, :, 1:]], axis=-1)


def _extract(x, rows, exact):
    m = jnp.max(x, axis=0, keepdims=True)
    hit = x == m
    if exact:
        first = jnp.min(jnp.where(hit, rows, float(x.shape[0])), axis=0, keepdims=True)
        hit = rows == first
    return m, hit


def _row_index(shape):
    return lax.broadcasted_iota(jnp.int32, shape, 0).astype(F32)


def _top_rows_ranked(x, dst_ref, k, exact):
    rows = _row_index(x.shape) if exact else None
    rank = jnp.full(x.shape, float(k), F32)
    for i in range(k):
        m, hit = _extract(x, rows, exact)
        dst_ref[i:i + 1, :] = m
        rank = jnp.where(hit, float(i), rank)
        x = jnp.where(hit, -jnp.inf, x)
    return rank, x


def _top2_rows(x1, x2, dst1_ref, dst2_ref, k, exact):
    rows = _row_index(x1.shape) if exact else None
    rank1 = jnp.full(x1.shape, float(k), F32) if exact else None
    rank2 = jnp.full(x2.shape, float(k), F32)
    for i in range(k):
        m1, hit1 = _extract(x1, rows, exact)
        m2, hit2 = _extract(x2, rows, exact)
        dst1_ref[i:i + 1, :] = m1
        dst2_ref[i:i + 1, :] = m2
        if exact:
            rank1 = jnp.where(hit1, float(i), rank1)
        x1 = jnp.where(hit1, -jnp.inf, x1)
        rank2 = jnp.where(hit2, float(i), rank2)
        x2 = jnp.where(hit2, -jnp.inf, x2)
    return rank1, rank2, x1, x2


def _count_removed(x):
    return jnp.sum(jnp.where(x == -jnp.inf, 1.0, 0.0), axis=0, keepdims=True)


def _dup_bf16_bits(x):
    bits = pltpu.bitcast(x.astype(BF16).astype(F32), jnp.uint32)
    return bits | (bits >> 16)


def _route_group(s1, s2, a_ref, b_ref, cand_ref, top_ref, exact):
    nk = s1.shape[0]
    rank1, rank2, x1, x2 = _top2_rows(s1, s2, a_ref, b_ref, PEER_TOPK, exact)
    offs, off = [], 0
    for r in range(PEER_TOPK):
        n_r = PEER_TOPK // (r + 1)
        cand_ref[off:off + n_r, :] = a_ref[r:r + 1, :] + b_ref[0:n_r, :]
        offs.append((off, n_r))
        off += n_r
    n_pad = cand_ref.shape[0] - off
    if n_pad:
        cand_ref[off:, :] = jnp.full((n_pad, LANES), -jnp.inf, F32)
    crank, xc = _top_rows_ranked(cand_ref[...], top_ref, PEER_TOPK, exact)
    taken = jnp.where(crank < float(PEER_TOPK), 1.0, 0.0)
    top = top_ref[...]
    z = jnp.sum(jnp.exp(top - top[0:1]), axis=0, keepdims=True)
    cnt = jnp.zeros((nk, LANES), F32)
    for r, (off, n_r) in enumerate(offs):
        cnt_r = jnp.sum(taken[off:off + n_r], axis=0, keepdims=True)
        is_r = (rank1 == float(r)) if exact else (s1 == a_ref[r:r + 1, :])
        cnt = jnp.where(is_r, cnt_r, cnt)
    c1 = jnp.exp(s1 - a_ref[0:1, :]) / z
    e2 = jnp.exp(s2 - b_ref[0:1, :])
    ties = None
    if not exact:
        k = float(PEER_TOPK)
        ties = ((_count_removed(x1) > k) | (_count_removed(x2) > k) | (_count_removed(xc) > k + n_pad))
    return cnt, c1, rank2, e2, ties


def _route_kernel(ht_ref, wq_ref, keys_ref, cnt_ref, c1_ref, r2_ref, e2_ref, a_ref, b_ref, cand_ref, top_ref):
    tm = ht_ref.shape[1]
    qt = jnp.dot(wq_ref[...], ht_ref[...], preferred_element_type=F32)
    half = qt.shape[0] // 2
    s1_all = jnp.dot(keys_ref[0], qt[:half].astype(BF16), preferred_element_type=F32)
    s2_all = jnp.dot(keys_ref[1], qt[half:].astype(BF16), preferred_element_type=F32)

    def store(c, cnt, c1, rank2, e2):
        lanes = slice(c * LANES, (c + 1) * LANES)
        cnt_ref[0, :, lanes] = _dup_bf16_bits(cnt)
        c1_ref[0, :, lanes] = _dup_bf16_bits(c1)
        r2_ref[0, :, lanes] = rank2.astype(BF16)
        e2_ref[0, :, lanes] = e2.astype(BF16)

    def group(c, exact):
        lanes = slice(c * LANES, (c + 1) * LANES)
        return _route_group(s1_all[:, lanes], s2_all[:, lanes], a_ref, b_ref, cand_ref, top_ref, exact)

    tie = []
    for c in range(tm // LANES):
        cnt, c1, rank2, e2, ties = group(c, exact=False)
        store(c, cnt, c1, rank2, e2)
        tie.append(jnp.where(ties, 1.0, 0.0))

    @pl.when(jnp.max(functools.reduce(jnp.maximum, tie)) > 0.0)
    def _():
        for c in range(tm // LANES):
            @pl.when(jnp.max(tie[c]) > 0.0)
            def _(c=c):
                store(c, *group(c, exact=True)[:4])


def _route(ht, wqt, keys, layer, *, tm=1024):
    d, t = ht.shape
    heads = PEER_HEADS
    dq2 = wqt.shape[1] // heads
    nk = keys.shape[2]
    tm = min(tm, t)
    n_cand = -(-sum(PEER_TOPK // (r + 1) for r in range(PEER_TOPK)) // 8) * 8
    row_shape = jax.ShapeDtypeStruct((heads, nk, t), jnp.uint32)
    row_spec = pl.BlockSpec((1, nk, tm), lambda i, h: (h, 0, i))
    tile_shape = jax.ShapeDtypeStruct((heads, nk, t), BF16)
    tile_spec = pl.BlockSpec((1, nk, tm), lambda i, h: (h, 0, i))
    return pl.pallas_call(
        _route_kernel,
        grid=(t // tm, heads),
        in_specs=[
            pl.BlockSpec((d, tm), lambda i, h: (0, i)),
            pl.BlockSpec((None, dq2, d), lambda i, h: (layer, h, 0)),
            pl.BlockSpec((None,) + keys.shape[1:], lambda i, h: (layer, 0, 0, 0)),
        ],
        out_specs=[row_spec, row_spec, tile_spec, tile_spec],
        out_shape=[row_shape, row_shape, tile_shape, tile_shape],
        scratch_shapes=[
            pltpu.VMEM((PEER_TOPK, LANES), F32),
            pltpu.VMEM((PEER_TOPK, LANES), F32),
            pltpu.VMEM((n_cand, LANES), F32),
            pltpu.VMEM((PEER_TOPK, LANES), F32),
        ],
        compiler_params=_params(("parallel", "arbitrary"), 40),
        name="peer_route",
    )(ht, wqt, keys)


def _gelu(x):
    return 0.5 * x * (1.0 + lax.erf(x * 0.7071067811865476))


GATE_LANES = 256
GATE_KGROUP = 2


def _row_as_bf16(row):
    return pltpu.bitcast(jnp.broadcast_to(row, (8, row.shape[1])), BF16)


def _build_gates(r2_ref, e2_ref, cnt_ref, c1_ref, row0, put, *, n1, heads):
    nk = r2_ref.shape[1]
    tm = r2_ref.shape[2]
    n_sub = nk // SUB16
    for k0 in range(0, n1, GATE_KGROUP):
        ks = range(k0, min(k0 + GATE_KGROUP, n1))
        for c in range(tm // GATE_LANES):
            lanes = slice(c * GATE_LANES, (c + 1) * GATE_LANES)
            g = {(k, s): jnp.zeros((SUB16, GATE_LANES), BF16) for k in ks for s in range(n_sub)}
            for h in range(heads):
                cnt = {k: _row_as_bf16(cnt_ref[h, row0 + k:row0 + k + 1, lanes]) for k in ks}
                c1 = {k: _row_as_bf16(c1_ref[h, row0 + k:row0 + k + 1, lanes]) for k in ks}
                for s in range(n_sub):
                    rows = slice(s * SUB16, (s + 1) * SUB16)
                    r2 = r2_ref[h, rows, lanes]
                    e2 = e2_ref[h, rows, lanes]
                    for k in ks:
                        g[k, s] = g[k, s] + jnp.where(r2 < cnt[k], e2, jnp.zeros_like(e2)) * c1[k]
            for k in ks:
                for s in range(n_sub):
                    put(k, s, c, g[k, s])


def _experts_kernel(ht_ref, u_ref, vt_ref, cnt_ref, c1_ref, cntn_ref, c1n_ref, r2_ref, e2_ref, x_ref, gf_ref,
                    o_ref, acc_ref, ga_ref, a_ref, *, n1, heads, final_norm):
    j = pl.program_id(1)
    nk = r2_ref.shape[1]
    tm = ht_ref.shape[1]
    te = n1 * nk
    n_sub = nk // SUB16
    build = functools.partial(_build_gates, r2_ref, e2_ref, n1=n1, heads=heads)
    slot, next_slot = j % 2, (j + 1) % 2

    def put_slot(which):
        def put(k, s, c, tile):
            ga_ref[which, k * nk + s * SUB16:k * nk + (s + 1) * SUB16, c * GATE_LANES:(c + 1) * GATE_LANES] = tile
        return put

    @pl.when(j == 0)
    def _():
        acc_ref[...] = jnp.zeros_like(acc_ref)
        build(cnt_ref, c1_ref, 0, put_slot(0))

    act0 = jnp.dot(u_ref[0:te, :], ht_ref[...], preferred_element_type=F32)
    a_ref[0:te, :] = _gelu(act0).astype(BF16) * ga_ref[slot]
    build(cntn_ref, c1n_ref, 0, put_slot(next_slot))
    tiles = {}
    build(cnt_ref, c1_ref, n1, lambda k, s, c, tile: tiles.__setitem__((k, s, c), tile))
    gb = jnp.concatenate(
        [jnp.concatenate([tiles[k, s, c] for c in range(tm // GATE_LANES)], axis=1)
         for k in range(n1) for s in range(n_sub)], axis=0)
    act1 = jnp.dot(u_ref[te:2 * te, :], ht_ref[...], preferred_element_type=F32)
    a_ref[te:2 * te, :] = _gelu(act1).astype(BF16) * gb
    acc_ref[...] += jnp.dot(vt_ref[...], a_ref[...], preferred_element_type=F32)

    @pl.when(j == pl.num_programs(1) - 1)
    def _():
        y = x_ref[...] + acc_ref[...].T
        o_ref[...] = _rms(y, gf_ref[...]) if final_norm else y


def _experts(ht, u, vt, layer, cnt, c1, r2, e2, x, gain_final=None, *, tm=512, n1=4):
    d, t = ht.shape
    heads, nk, _ = cnt.shape
    n_e = u.shape[1]
    tm = min(tm, t)
    te = n1 * nk
    n_j = n_e // (2 * te)
    assert n_e == n_j * 2 * te and (2 * n1) % 8 == 0
    rows_now = pl.BlockSpec((heads, 2 * n1, tm), lambda i, j: (0, j, i))
    rows_next = pl.BlockSpec((heads, 2 * n1, tm), lambda i, j: (0, jnp.minimum(j + 1, n_j - 1), i))
    tile_spec = pl.BlockSpec((heads, nk, tm), lambda i, j: (0, 0, i))
    return pl.pallas_call(
        functools.partial(_experts_kernel, n1=n1, heads=heads, final_norm=gain_final is not None),
        grid=(t // tm, n_j),
        in_specs=[
            pl.BlockSpec((d, tm), lambda i, j: (0, i)),
            pl.BlockSpec((None, 2 * te, d), lambda i, j: (layer, j, 0)),
            pl.BlockSpec((None, d, 2 * te), lambda i, j: (layer, 0, j)),
            rows_now, rows_now, rows_next, rows_next, tile_spec, tile_spec,
            pl.BlockSpec((tm, d), lambda i, j: (i, 0)),
            pl.BlockSpec((1, d), lambda i, j: (0, 0)),
        ],
        out_specs=pl.BlockSpec((tm, d), lambda i, j: (i, 0)),
        out_shape=jax.ShapeDtypeStruct((t, d), F32),
        scratch_shapes=[
            pltpu.VMEM((d, tm), F32),
            pltpu.VMEM((2, te, tm), BF16),
            pltpu.VMEM((2 * te, tm), BF16),
        ],
        compiler_params=_params(("parallel", "arbitrary"), 56),
        name="peer_experts",
    )(ht, u, vt, cnt, c1, cnt, c1, r2, e2, x,
      (jnp.ones((d,), F32) if gain_final is None else gain_final).reshape(1, d).astype(F32))


def _swap_halves(w):
    half = w.shape[-1] // 2
    return jnp.concatenate([w[..., half:], w[..., :half]], axis=-1)


def _prep_w_in(w_in, q_lora, kv_lora, nat_width):
    o1 = q_lora
    o2 = o1 + kv_lora
    o3 = o2 + MLA_ROPE
    w_in = w_in.astype(BF16)
    c_q, c_kv, k_pe, nat = w_in[..., :o1], w_in[..., o1:o2], w_in[..., o2:o3], w_in[..., o3:]
    assert nat.shape[-1] == 3 * nat_width
    k_sw = _swap_halves(k_pe)
    return jnp.concatenate([nat, c_q, c_kv, k_pe, k_pe, k_sw, k_sw], axis=-1)


def _prep_w_uq(w_uq):
    n_l, k, _ = w_uq.shape
    w = w_uq.reshape(n_l, k, MLA_HEADS, MLA_NOPE + MLA_ROPE)
    nope = w[..., :MLA_NOPE].reshape(n_l, k, MLA_HEADS * MLA_NOPE)
    pe = w[..., MLA_NOPE:]
    return jnp.concatenate([nope, pe.reshape(n_l, k, -1), _swap_halves(pe).reshape(n_l, k, -1)], axis=-1).astype(BF16)


def _prep_w_ukv(w_ukv):
    n_l, k, n = w_ukv.shape
    w = w_ukv.reshape(n_l, k, MLA_HEADS, n // MLA_HEADS)
    return jnp.concatenate([w[..., :MLA_NOPE].reshape(n_l, k, -1), w[..., MLA_NOPE:].reshape(n_l, k, -1)],
                           axis=-1).astype(BF16)


def _rope_tables(seq):
    inv = ROPE_BASE ** (-jnp.arange(0, MLA_ROPE, 2, dtype=F32) / MLA_ROPE)
    ang = jnp.arange(seq, dtype=F32)[:, None] * inv[None, :]
    cos, sin = jnp.cos(ang), jnp.sin(ang)
    return jnp.concatenate([cos, cos, cos, cos], axis=-1), jnp.concatenate([-sin, sin, -sin, sin], axis=-1)


def kernel(x, attn_norm, w_in, mla_q_norm, mla_w_uq, mla_kv_norm, mla_w_ukv, nat_rpb, mla_out_norm, nat_out_norm,
           w_out, ffn_norm, peer_w_q, peer_sub_keys, peer_u, peer_v, final_norm):
    batch, seq, d = x.shape
    depth = w_in.shape[0]
    q_lora, kv_lora = mla_q_norm.shape[1], mla_kv_norm.shape[1]
    nat_width = nat_rpb.shape[1] * NAT_HEAD_DIM
    mla_width = mla_out_norm.shape[1]
    assert mla_w_ukv.shape[2] == MLA_HEADS * 2 * LANES and mla_width == MLA_HEADS * LANES
    assert nat_width == mla_width and q_lora % LANES == 0 and kv_lora % LANES == 0
    t = batch * seq

    w1 = _prep_w_in(w_in, q_lora, kv_lora, nat_width)
    cq_col = 3 * nat_width // q_lora
    ckv_col = (3 * nat_width + q_lora) // kv_lora
    kpe_col = (3 * nat_width + q_lora + kv_lora) // LANES
    wq = _prep_w_uq(mla_w_uq)
    wkv = _prep_w_ukv(mla_w_ukv)
    wo = w_out.astype(BF16)
    wpq_t = jnp.swapaxes(peer_w_q, 1, 2).astype(BF16)
    keys = peer_sub_keys.astype(BF16)
    u = peer_u.astype(BF16)
    vt = jnp.swapaxes(peer_v, 1, 2).astype(BF16)
    bias_tbl = _nat_bias_tables(nat_rpb)
    cc, ss = _rope_tables(seq)

    xf = x.reshape(t, d)
    for l in range(depth):
        proj = _matmul([(xf, 0, d)], w1, l, out_dtype=BF16, gain=attn_norm[l], tn=2048, name="proj_in")
        q = _matmul([(proj, cq_col, q_lora)], wq, l, out_dtype=BF16, gain=mla_q_norm[l], tn=2048, name="mla_q_up")
        kv = _matmul([(proj, ckv_col, kv_lora)], wkv, l, out_dtype=BF16, gain=mla_kv_norm[l], tn=2048,
                     name="mla_kv_up")
        mla_o = _mla(q, kv, proj, cc, ss, mla_out_norm[l], batch=batch, seq=seq, kpe_col=kpe_col)
        nat_o = _nat(proj, bias_tbl, l, nat_out_norm[l], batch=batch, seq=seq)
        xf, ht = _matmul([(mla_o, 0, mla_width), (nat_o, 0, nat_width)], wo, l, out_dtype=F32, res=xf,
                         gain_out=ffn_norm[l], tn=d, name="mix_out")
        cnt, c1, r2, e2 = _route(ht, wpq_t, keys, l)
        xf = _experts(ht, u, vt, l, cnt, c1, r2, e2, xf, final_norm if l == depth - 1 else None)
    return xf.reshape(batch, seq, d)
```

```python
import functools

import numpy as np
import jax
import jax.numpy as jnp
from jax import lax
from jax.experimental import pallas as pl
from jax.experimental.pallas import tpu as pltpu

F32 = jnp.float32
BF16 = jnp.bfloat16

RMS_EPS = 1e-6
GRID_W = 64
MLA_HEADS = 8
MLA_NOPE = 128
MLA_ROPE = 64
ROPE_BASE = 10000.0
NAT_HEAD_DIM = 64
NAT_WIN_R = 8
NAT_WIN_C = 16
PEER_HEADS = 8
PEER_TOPK = 16
LANES = 128
SUB16 = 16
NEG_BIG = -1e30
MIB = 1024 * 1024


def _params(sem, vmem_mib):
    return pltpu.CompilerParams(dimension_semantics=sem, vmem_limit_bytes=vmem_mib * MIB)


def _rms(x, g):
    ms = jnp.mean(x * x, axis=-1, keepdims=True)
    return x * lax.rsqrt(ms + RMS_EPS) * g


def _rmsnorm_kernel(x_ref, g_ref, o_ref, *, transpose):
    y = _rms(x_ref[...].astype(F32), g_ref[...])
    if transpose:
        y = y.T
    o_ref[...] = y.astype(o_ref.dtype)


def _rmsnorm(x, g, *, out_dtype, transpose=False, tm=512):
    t, d = x.shape
    tm = min(tm, t)
    if transpose:
        out_shape = jax.ShapeDtypeStruct((d, t), out_dtype)
        out_spec = pl.BlockSpec((d, tm), lambda i: (0, i))
    else:
        out_shape = jax.ShapeDtypeStruct((t, d), out_dtype)
        out_spec = pl.BlockSpec((tm, d), lambda i: (i, 0))
    return pl.pallas_call(
        functools.partial(_rmsnorm_kernel, transpose=transpose),
        grid=(t // tm,),
        in_specs=[pl.BlockSpec((tm, d), lambda i: (i, 0)), pl.BlockSpec((1, d), lambda i: (0, 0))],
        out_specs=out_spec,
        out_shape=out_shape,
        compiler_params=_params(("parallel",), 40),
        name="rmsnorm_t" if transpose else "rmsnorm",
    )(x, g.reshape(1, d).astype(F32))


def _mm_kernel(*refs, n_a, norm, has_res, norm_out):
    a_refs = refs[:n_a]
    pos = n_a
    g_ref = refs[pos] if norm else None
    pos += int(norm)
    w_ref = refs[pos]
    pos += 1
    res_ref = refs[pos] if has_res else None
    pos += int(has_res)
    g2_ref = refs[pos] if norm_out else None
    pos += int(norm_out)
    o_ref = refs[pos]
    pos += 1
    ot_ref = refs[pos] if norm_out else None
    pos += int(norm_out)
    an_ref = refs[pos]

    @pl.when(pl.program_id(1) == 0)
    def _():
        parts = [r[...] for r in a_refs]
        a = parts[0] if n_a == 1 else jnp.concatenate(parts, axis=1)
        if norm:
            a = _rms(a.astype(F32), g_ref[...])
        an_ref[...] = a.astype(BF16)

    acc = jnp.dot(an_ref[...], w_ref[...], preferred_element_type=F32)
    if has_res:
        acc = acc + res_ref[...]
    o_ref[...] = acc.astype(o_ref.dtype)
    if norm_out:
        ot_ref[...] = _rms(acc, g2_ref[...]).T.astype(ot_ref.dtype)


def _matmul(a_list, w, layer, *, out_dtype, gain=None, res=None, gain_out=None, tm=512, tn=512, name="matmul"):
    t = a_list[0][0].shape[0]
    _, k, n = w.shape
    assert k == sum(width for _, _, width in a_list)
    tm, tn = min(tm, t), min(tn, n)
    in_specs, args = [], []
    for arr, cb, width in a_list:
        in_specs.append(pl.BlockSpec((tm, width), lambda i, j, cb=cb: (i, cb)))
        args.append(arr)
    if gain is not None:
        in_specs.append(pl.BlockSpec((1, k), lambda i, j: (0, 0)))
        args.append(gain.reshape(1, k).astype(F32))
    in_specs.append(pl.BlockSpec((None, k, tn), lambda i, j: (layer, 0, j)))
    args.append(w)
    if res is not None:
        in_specs.append(pl.BlockSpec((tm, tn), lambda i, j: (i, j)))
        args.append(res)
    out_specs = pl.BlockSpec((tm, tn), lambda i, j: (i, j))
    out_shape = jax.ShapeDtypeStruct((t, n), out_dtype)
    if gain_out is not None:
        assert tn == n
        in_specs.append(pl.BlockSpec((1, n), lambda i, j: (0, 0)))
        args.append(gain_out.reshape(1, n).astype(F32))
        out_specs = [out_specs, pl.BlockSpec((n, tm), lambda i, j: (0, i))]
        out_shape = [out_shape, jax.ShapeDtypeStruct((n, t), BF16)]
    return pl.pallas_call(
        functools.partial(_mm_kernel, n_a=len(a_list), norm=gain is not None, has_res=res is not None,
                          norm_out=gain_out is not None),
        grid=(t // tm, n // tn),
        in_specs=in_specs,
        out_specs=out_specs,
        out_shape=out_shape,
        scratch_shapes=[pltpu.VMEM((tm, k), BF16)],
        compiler_params=_params(("parallel", "arbitrary"), 48),
        name=name,
    )(*args)


def _mla_kernel(q_ref, k_ref, v_ref, ka_ref, kb_ref, cck_ref, ssk_ref, ccq_ref, ssq_ref, g_ref, o_ref, kcat_ref,
                s_ref, p_ref, *, heads, scale):
    @pl.when(pl.program_id(1) == 0)
    def _():
        rk = ka_ref[...].astype(F32) * cck_ref[...] + kb_ref[...].astype(F32) * ssk_ref[...]
        rk = rk.astype(BF16)
        for h in range(heads):
            kcat_ref[h, :, 0:LANES] = k_ref[:, h * LANES:(h + 1) * LANES]
            kcat_ref[h, :, LANES:2 * LANES] = rk

    tq = q_ref.shape[0]
    pe0 = heads * MLA_NOPE
    sw0 = pe0 + (heads // 2) * LANES
    ccq, ssq = ccq_ref[...], ssq_ref[...]
    lane = lax.broadcasted_iota(jnp.int32, (tq, LANES), 1)

    def scores(h):
        p_lo = pe0 + (h // 2) * LANES
        s_lo = sw0 + (h // 2) * LANES
        rq = q_ref[:, p_lo:p_lo + LANES].astype(F32) * ccq + q_ref[:, s_lo:s_lo + LANES].astype(F32) * ssq
        keep = (lane < MLA_ROPE) if h % 2 == 0 else (lane >= MLA_ROPE)
        rq = jnp.where(keep, rq, 0.0).astype(BF16)
        qcat = jnp.concatenate([q_ref[:, h * LANES:(h + 1) * LANES], rq], axis=1)
        s_ref[h % 2] = lax.dot_general(qcat, kcat_ref[h], (((1,), (1,)), ((), ())), preferred_element_type=F32)

    def softmax(h):
        c = scale * 1.4426950408889634
        inv = []
        for r in range(tq // SUB16):
            rows = slice(r * SUB16, (r + 1) * SUB16)
            s = s_ref[h % 2, rows, :]
            m = jnp.max(s, axis=-1, keepdims=True)
            p = jnp.exp2((s - m) * c)
            inv.append(1.0 / jnp.sum(p, axis=-1, keepdims=True))
            p_ref[h % 2, rows, :] = p.astype(BF16)
        return jnp.concatenate(inv, axis=0)

    def values(h, inv_l):
        return jnp.dot(p_ref[h % 2], v_ref[:, h * LANES:(h + 1) * LANES], preferred_element_type=F32) * inv_l

    outs, inv_l = [], {}
    scores(0)
    for h in range(heads):
        if h + 1 < heads:
            scores(h + 1)
        inv_l[h] = softmax(h)
        if h >= 1:
            outs.append(values(h - 1, inv_l[h - 1]))
    outs.append(values(heads - 1, inv_l[heads - 1]))
    o_all = jnp.concatenate(outs, axis=1)
    o_ref[...] = _rms(o_all, g_ref[...]).astype(o_ref.dtype)


def _mla(q, kv, proj, cc, ss, gain, *, batch, seq, kpe_col, tq=512):
    heads = MLA_HEADS
    t = q.shape[0]
    tq = min(tq, seq)
    nq = seq // tq
    width = heads * LANES
    return pl.pallas_call(
        functools.partial(_mla_kernel, heads=heads, scale=float((MLA_NOPE + MLA_ROPE) ** -0.5)),
        grid=(batch, nq),
        in_specs=[
            pl.BlockSpec((tq, q.shape[1]), lambda b, i: (b * nq + i, 0)),
            pl.BlockSpec((seq, width), lambda b, i: (b, 0)),
            pl.BlockSpec((seq, width), lambda b, i: (b, 1)),
            pl.BlockSpec((seq, LANES), lambda b, i: (b, kpe_col)),
            pl.BlockSpec((seq, LANES), lambda b, i: (b, kpe_col + 1)),
            pl.BlockSpec((seq, LANES), lambda b, i: (0, 0)),
            pl.BlockSpec((seq, LANES), lambda b, i: (0, 0)),
            pl.BlockSpec((tq, LANES), lambda b, i: (i, 0)),
            pl.BlockSpec((tq, LANES), lambda b, i: (i, 0)),
            pl.BlockSpec((1, width), lambda b, i: (0, 0)),
        ],
        out_specs=pl.BlockSpec((tq, width), lambda b, i: (b * nq + i, 0)),
        out_shape=jax.ShapeDtypeStruct((t, width), BF16),
        scratch_shapes=[
            pltpu.VMEM((heads, seq, 2 * LANES), BF16),
            pltpu.VMEM((2, tq, seq), F32),
            pltpu.VMEM((2, tq, seq), BF16),
        ],
        compiler_params=_params(("parallel", "arbitrary"), 56),
        name="mla_attention",
    )(q, kv, kv, proj, proj, cc, ss, cc, ss, gain.reshape(1, width).astype(F32))


def _nat_kernel(q_ref, k_ref, v_ref, bias_ref, g_ref, o_ref, s_ref, p_ref, *, rows, pairs):
    r = pl.program_id(1)
    rs = jnp.clip(r - NAT_WIN_R // 2, 0, rows - NAT_WIN_R)
    start = rs - r + (NAT_WIN_R - 1)
    koff = pl.multiple_of(rs * GRID_W, GRID_W)
    band = NAT_WIN_R * GRID_W
    lane = lax.broadcasted_iota(jnp.int32, (GRID_W, LANES), 1)
    first = lane < NAT_HEAD_DIM
    for j in range(pairs):
        cols = slice(j * LANES, (j + 1) * LANES)
        q2 = q_ref[:, cols] * jnp.asarray(NAT_HEAD_DIM ** -0.5, BF16)
        zero = jnp.zeros_like(q2)
        qq = jnp.concatenate([jnp.where(first, q2, zero), jnp.where(first, zero, q2)], axis=0)
        kb = k_ref[pl.ds(koff, band), cols]
        s_ref[j] = lax.dot_general(qq, kb, (((1,), (1,)), ((), ())), preferred_element_type=F32)
    inv_l = []
    for j in range(pairs):
        bias = jnp.concatenate([bias_ref[j, start + 2 * m] for m in range(NAT_WIN_R // 2)], axis=1)
        s = s_ref[j] + bias
        m = jnp.max(s, axis=-1, keepdims=True)
        p = jnp.exp(s - m)
        inv_l.append(1.0 / jnp.sum(p, axis=-1, keepdims=True))
        p_ref[j] = p.astype(BF16)
    outs = []
    for j in range(pairs):
        cols = slice(j * LANES, (j + 1) * LANES)
        vb = v_ref[pl.ds(koff, band), cols]
        o = jnp.dot(p_ref[j], vb, preferred_element_type=F32) * inv_l[j]
        outs.append(jnp.where(first, o[:GRID_W], o[GRID_W:]))
    o_all = jnp.concatenate(outs, axis=1)
    o_ref[...] = _rms(o_all, g_ref[...]).astype(o_ref.dtype)


def _nat(proj, bias_tbl, layer, gain, *, batch, seq):
    rows = seq // GRID_W
    assert rows >= NAT_WIN_R
    t = proj.shape[0]
    pairs, n_off = bias_tbl.shape[1], bias_tbl.shape[2]
    width = pairs * LANES
    return pl.pallas_call(
        functools.partial(_nat_kernel, rows=rows, pairs=pairs),
        grid=(batch, rows),
        in_specs=[
            pl.BlockSpec((GRID_W, width), lambda b, r: (b * rows + r, 0)),
            pl.BlockSpec((seq, width), lambda b, r: (b, 1)),
            pl.BlockSpec((seq, width), lambda b, r: (b, 2)),
            pl.BlockSpec((None, pairs, n_off, 2 * GRID_W, 2 * GRID_W), lambda b, r: (layer, 0, 0, 0, 0)),
            pl.BlockSpec((1, width), lambda b, r: (0, 0)),
        ],
        out_specs=pl.BlockSpec((GRID_W, width), lambda b, r: (b * rows + r, 0)),
        out_shape=jax.ShapeDtypeStruct((t, width), BF16),
        scratch_shapes=[
            pltpu.VMEM((pairs, 2 * GRID_W, NAT_WIN_R * GRID_W), F32),
            pltpu.VMEM((pairs, 2 * GRID_W, NAT_WIN_R * GRID_W), BF16),
        ],
        compiler_params=_params(("parallel", "arbitrary"), 48),
        name="nat_attention",
    )(proj, proj, proj, bias_tbl, gain.reshape(1, width).astype(F32))


def _nat_bias_tables(rpb):
    n_l, heads, n_off = rpb.shape[0], rpb.shape[1], rpb.shape[2]
    qc = np.arange(GRID_W)[:, None]
    kc = np.arange(GRID_W)[None, :]
    c_start = np.clip(qc - NAT_WIN_C // 2, 0, GRID_W - NAT_WIN_C)
    valid = (kc >= c_start) & (kc < c_start + NAT_WIN_C)
    dc_idx = np.clip(kc - qc, -(NAT_WIN_C - 1), NAT_WIN_C - 1) + (NAT_WIN_C - 1)
    tbl = rpb.astype(F32)[:, :, :, dc_idx]
    tbl = jnp.where(jnp.asarray(valid), tbl, NEG_BIG)
    tbl = tbl.reshape(n_l, heads // 2, 2, n_off, GRID_W, GRID_W)
    tbl = jnp.transpose(tbl, (0, 1, 3, 2, 4, 5)).reshape(n_l, heads // 2, n_off, 2 * GRID_W, GRID_W)
    return jnp.concatenate([tbl[:, :, :-1], tbl[:, :, 1:]], axis=-1)


def _extract(x, rows, exact):
    m = jnp.max(x, axis=0, keepdims=True)
    hit = x == m
    if exact:
        first = jnp.min(jnp.where(hit, rows, float(x.shape[0])), axis=0, keepdims=True)
        hit = rows == first
    return m, hit


def _row_index(shape):
    return lax.broadcasted_iota(jnp.int32, shape, 0).astype(F32)


def _top_rows_ranked(x, dst_ref, k, exact):
    rows = _row_index(x.shape) if exact else None
    rank = jnp.full(x.shape, float(k), F32)
    for i in range(k):
        m, hit = _extract(x, rows, exact)
        dst_ref[i:i + 1, :] = m
        rank = jnp.where(hit, float(i), rank)
        x = jnp.where(hit, -jnp.inf, x)
    return rank, x


def _top2_rows(x1, x2, dst1_ref, dst2_ref, k, exact):
    rows = _row_index(x1.shape) if exact else None
    rank1 = jnp.full(x1.shape, float(k), F32) if exact else None
    rank2 = jnp.full(x2.shape, float(k), F32)
    for i in range(k):
        m1, hit1 = _extract(x1, rows, exact)
        m2, hit2 = _extract(x2, rows, exact)
        dst1_ref[i:i + 1, :] = m1
        dst2_ref[i:i + 1, :] = m2
        if exact:
            rank1 = jnp.where(hit1, float(i), rank1)
        x1 = jnp.where(hit1, -jnp.inf, x1)
        rank2 = jnp.where(hit2, float(i), rank2)
        x2 = jnp.where(hit2, -jnp.inf, x2)
    return rank1, rank2, x1, x2


def _count_removed(x):
    return jnp.sum(jnp.where(x == -jnp.inf, 1.0, 0.0), axis=0, keepdims=True)


def _dup_bf16_bits(x):
    bits = pltpu.bitcast(x.astype(BF16).astype(F32), jnp.uint32)
    return bits | (bits >> 16)


def _route_group(s1, s2, a_ref, b_ref, cand_ref, top_ref, exact):
    nk = s1.shape[0]
    rank1, rank2, x1, x2 = _top2_rows(s1, s2, a_ref, b_ref, PEER_TOPK, exact)
    offs, off = [], 0
    for r in range(PEER_TOPK):
        n_r = PEER_TOPK // (r + 1)
        cand_ref[off:off + n_r, :] = a_ref[r:r + 1, :] + b_ref[0:n_r, :]
        offs.append((off, n_r))
        off += n_r
    n_pad = cand_ref.shape[0] - off
    if n_pad:
        cand_ref[off:, :] = jnp.full((n_pad, LANES), -jnp.inf, F32)
    crank, xc = _top_rows_ranked(cand_ref[...], top_ref, PEER_TOPK, exact)
    taken = jnp.where(crank < float(PEER_TOPK), 1.0, 0.0)
    top = top_ref[...]
    z = jnp.sum(jnp.exp(top - top[0:1]), axis=0, keepdims=True)
    cnt = jnp.zeros((nk, LANES), F32)
    for r, (off, n_r) in enumerate(offs):
        cnt_r = jnp.sum(taken[off:off + n_r], axis=0, keepdims=True)
        is_r = (rank1 == float(r)) if exact else (s1 == a_ref[r:r + 1, :])
        cnt = jnp.where(is_r, cnt_r, cnt)
    c1 = jnp.exp(s1 - a_ref[0:1, :]) / z
    e2 = jnp.exp(s2 - b_ref[0:1, :])
    ties = None
    if not exact:
        k = float(PEER_TOPK)
        ties = ((_count_removed(x1) > k) | (_count_removed(x2) > k) | (_count_removed(xc) > k + n_pad))
    return cnt, c1, rank2, e2, ties


def _route_kernel(ht_ref, wq_ref, keys_ref, cnt_ref, c1_ref, r2_ref, e2_ref, a_ref, b_ref, cand_ref, top_ref):
    tm = ht_ref.shape[1]
    qt = jnp.dot(wq_ref[...], ht_ref[...], preferred_element_type=F32)
    half = qt.shape[0] // 2
    s1_all = jnp.dot(keys_ref[0], qt[:half].astype(BF16), preferred_element_type=F32)
    s2_all = jnp.dot(keys_ref[1], qt[half:].astype(BF16), preferred_element_type=F32)

    def store(c, cnt, c1, rank2, e2):
        lanes = slice(c * LANES, (c + 1) * LANES)
        cnt_ref[0, :, lanes] = _dup_bf16_bits(cnt)
        c1_ref[0, :, lanes] = _dup_bf16_bits(c1)
        r2_ref[0, :, lanes] = rank2.astype(BF16)
        e2_ref[0, :, lanes] = e2.astype(BF16)

    def group(c, exact):
        lanes = slice(c * LANES, (c + 1) * LANES)
        return _route_group(s1_all[:, lanes], s2_all[:, lanes], a_ref, b_ref, cand_ref, top_ref, exact)

    tie = []
    for c in range(tm // LANES):
        cnt, c1, rank2, e2, ties = group(c, exact=False)
        store(c, cnt, c1, rank2, e2)
        tie.append(jnp.where(ties, 1.0, 0.0))

    @pl.when(jnp.max(functools.reduce(jnp.maximum, tie)) > 0.0)
    def _():
        for c in range(tm // LANES):
            @pl.when(jnp.max(tie[c]) > 0.0)
            def _(c=c):
                store(c, *group(c, exact=True)[:4])


def _route(ht, wqt, keys, layer, *, tm=1024):
    d, t = ht.shape
    heads = PEER_HEADS
    dq2 = wqt.shape[1] // heads
    nk = keys.shape[2]
    tm = min(tm, t)
    n_cand = -(-sum(PEER_TOPK // (r + 1) for r in range(PEER_TOPK)) // 8) * 8
    row_shape = jax.ShapeDtypeStruct((heads, nk, t), jnp.uint32)
    row_spec = pl.BlockSpec((1, nk, tm), lambda i, h: (h, 0, i))
    tile_shape = jax.ShapeDtypeStruct((heads, nk, t), BF16)
    tile_spec = pl.BlockSpec((1, nk, tm), lambda i, h: (h, 0, i))
    return pl.pallas_call(
        _route_kernel,
        grid=(t // tm, heads),
        in_specs=[
            pl.BlockSpec((d, tm), lambda i, h: (0, i)),
            pl.BlockSpec((None, dq2, d), lambda i, h: (layer, h, 0)),
            pl.BlockSpec((None,) + keys.shape[1:], lambda i, h: (layer, 0, 0, 0)),
        ],
        out_specs=[row_spec, row_spec, tile_spec, tile_spec],
        out_shape=[row_shape, row_shape, tile_shape, tile_shape],
        scratch_shapes=[
            pltpu.VMEM((PEER_TOPK, LANES), F32),
            pltpu.VMEM((PEER_TOPK, LANES), F32),
            pltpu.VMEM((n_cand, LANES), F32),
            pltpu.VMEM((PEER_TOPK, LANES), F32),
        ],
        compiler_params=_params(("parallel", "arbitrary"), 40),
        name="peer_route",
    )(ht, wqt, keys)


def _gelu(x):
    return 0.5 * x * (1.0 + lax.erf(x * 0.7071067811865476))


GATE_LANES = 256
GATE_KGROUP = 2


def _row_as_bf16(row):
    return pltpu.bitcast(jnp.broadcast_to(row, (8, row.shape[1])), BF16)


def _build_gates(r2_ref, e2_ref, cnt_ref, c1_ref, row0, put, *, n1, heads):
    nk = r2_ref.shape[1]
    tm = r2_ref.shape[2]
    n_sub = nk // SUB16
    for k0 in range(0, n1, GATE_KGROUP):
        ks = range(k0, min(k0 + GATE_KGROUP, n1))
        for c in range(tm // GATE_LANES):
            lanes = slice(c * GATE_LANES, (c + 1) * GATE_LANES)
            g = {(k, s): jnp.zeros((SUB16, GATE_LANES), BF16) for k in ks for s in range(n_sub)}
            for h in range(heads):
                cnt = {k: _row_as_bf16(cnt_ref[h, row0 + k:row0 + k + 1, lanes]) for k in ks}
                c1 = {k: _row_as_bf16(c1_ref[h, row0 + k:row0 + k + 1, lanes]) for k in ks}
                for s in range(n_sub):
                    rows = slice(s * SUB16, (s + 1) * SUB16)
                    r2 = r2_ref[h, rows, lanes]
                    e2 = e2_ref[h, rows, lanes]
                    for k in ks:
                        g[k, s] = g[k, s] + jnp.where(r2 < cnt[k], e2, jnp.zeros_like(e2)) * c1[k]
            for k in ks:
                for s in range(n_sub):
                    put(k, s, c, g[k, s])


def _experts_kernel(ht_ref, u_ref, vt_ref, cnt_ref, c1_ref, cntn_ref, c1n_ref, r2_ref, e2_ref, x_ref, gf_ref,
                    o_ref, acc_ref, ga_ref, a_ref, *, n1, heads, final_norm):
    j = pl.program_id(1)
    nk = r2_ref.shape[1]
    tm = ht_ref.shape[1]
    te = n1 * nk
    n_sub = nk // SUB16
    build = functools.partial(_build_gates, r2_ref, e2_ref, n1=n1, heads=heads)
    slot, next_slot = j % 2, (j + 1) % 2

    def put_slot(which):
        def put(k, s, c, tile):
            ga_ref[which, k * nk + s * SUB16:k * nk + (s + 1) * SUB16, c * GATE_LANES:(c + 1) * GATE_LANES] = tile
        return put

    @pl.when(j == 0)
    def _():
        acc_ref[...] = jnp.zeros_like(acc_ref)
        build(cnt_ref, c1_ref, 0, put_slot(0))

    act0 = jnp.dot(u_ref[0:te, :], ht_ref[...], preferred_element_type=F32)
    a_ref[0:te, :] = _gelu(act0).astype(BF16) * ga_ref[slot]
    build(cntn_ref, c1n_ref, 0, put_slot(next_slot))
    tiles = {}
    build(cnt_ref, c1_ref, n1, lambda k, s, c, tile: tiles.__setitem__((k, s, c), tile))
    gb = jnp.concatenate(
        [jnp.concatenate([tiles[k, s, c] for c in range(tm // GATE_LANES)], axis=1)
         for k in range(n1) for s in range(n_sub)], axis=0)
    act1 = jnp.dot(u_ref[te:2 * te, :], ht_ref[...], preferred_element_type=F32)
    a_ref[te:2 * te, :] = _gelu(act1).astype(BF16) * gb
    acc_ref[...] += jnp.dot(vt_ref[...], a_ref[...], preferred_element_type=F32)

    @pl.when(j == pl.num_programs(1) - 1)
    def _():
        y = x_ref[...] + acc_ref[...].T
        o_ref[...] = _rms(y, gf_ref[...]) if final_norm else y


def _experts(ht, u, vt, layer, cnt, c1, r2, e2, x, gain_final=None, *, tm=512, n1=4):
    d, t = ht.shape
    heads, nk, _ = cnt.shape
    n_e = u.shape[1]
    tm = min(tm, t)
    te = n1 * nk
    n_j = n_e // (2 * te)
    assert n_e == n_j * 2 * te and (2 * n1) % 8 == 0
    rows_now = pl.BlockSpec((heads, 2 * n1, tm), lambda i, j: (0, j, i))
    rows_next = pl.BlockSpec((heads, 2 * n1, tm), lambda i, j: (0, jnp.minimum(j + 1, n_j - 1), i))
    tile_spec = pl.BlockSpec((heads, nk, tm), lambda i, j: (0, 0, i))
    return pl.pallas_call(
        functools.partial(_experts_kernel, n1=n1, heads=heads, final_norm=gain_final is not None),
        grid=(t // tm, n_j),
        in_specs=[
            pl.BlockSpec((d, tm), lambda i, j: (0, i)),
            pl.BlockSpec((None, 2 * te, d), lambda i, j: (layer, j, 0)),
            pl.BlockSpec((None, d, 2 * te), lambda i, j: (layer, 0, j)),
            rows_now, rows_now, rows_next, rows_next, tile_spec, tile_spec,
            pl.BlockSpec((tm, d), lambda i, j: (i, 0)),
            pl.BlockSpec((1, d), lambda i, j: (0, 0)),
        ],
        out_specs=pl.BlockSpec((tm, d), lambda i, j: (i, 0)),
        out_shape=jax.ShapeDtypeStruct((t, d), F32),
        scratch_shapes=[
            pltpu.VMEM((d, tm), F32),
            pltpu.VMEM((2, te, tm), BF16),
            pltpu.VMEM((2 * te, tm), BF16),
        ],
        compiler_params=_params(("parallel", "arbitrary"), 56),
        name="peer_experts",
    )(ht, u, vt, cnt, c1, cnt, c1, r2, e2, x,
      (jnp.ones((d,), F32) if gain_final is None else gain_final).reshape(1, d).astype(F32))


def _swap_halves(w):
    half = w.shape[-1] // 2
    return jnp.concatenate([w[..., half:], w[..., :half]], axis=-1)


def _prep_w_in(w_in, q_lora, kv_lora, nat_width):
    o1 = q_lora
    o2 = o1 + kv_lora
    o3 = o2 + MLA_ROPE
    w_in = w_in.astype(BF16)
    c_q, c_kv, k_pe, nat = w_in[..., :o1], w_in[..., o1:o2], w_in[..., o2:o3], w_in[..., o3:]
    assert nat.shape[-1] == 3 * nat_width
    k_sw = _swap_halves(k_pe)
    return jnp.concatenate([nat, c_q, c_kv, k_pe, k_pe, k_sw, k_sw], axis=-1)


def _prep_w_uq(w_uq):
    n_l, k, _ = w_uq.shape
    w = w_uq.reshape(n_l, k, MLA_HEADS, MLA_NOPE + MLA_ROPE)
    nope = w[..., :MLA_NOPE].reshape(n_l, k, MLA_HEADS * MLA_NOPE)
    pe = w[..., MLA_NOPE:]
    return jnp.concatenate([nope, pe.reshape(n_l, k, -1), _swap_halves(pe).reshape(n_l, k, -1)], axis=-1).astype(BF16)


def _prep_w_ukv(w_ukv):
    n_l, k, n = w_ukv.shape
    w = w_ukv.reshape(n_l, k, MLA_HEADS, n // MLA_HEADS)
    return jnp.concatenate([w[..., :MLA_NOPE].reshape(n_l, k, -1), w[..., MLA_NOPE:].reshape(n_l, k, -1)],
                           axis=-1).astype(BF16)


def _rope_tables(seq):
    inv = ROPE_BASE ** (-jnp.arange(0, MLA_ROPE, 2, dtype=F32) / MLA_ROPE)
    ang = jnp.arange(seq, dtype=F32)[:, None] * inv[None, :]
    cos, sin = jnp.cos(ang), jnp.sin(ang)
    return jnp.concatenate([cos, cos, cos, cos], axis=-1), jnp.concatenate([-sin, sin, -sin, sin], axis=-1)


def kernel(x, attn_norm, w_in, mla_q_norm, mla_w_uq, mla_kv_norm, mla_w_ukv, nat_rpb, mla_out_norm, nat_out_norm,
           w_out, ffn_norm, peer_w_q, peer_sub_keys, peer_u, peer_v, final_norm):
    batch, seq, d = x.shape
    depth = w_in.shape[0]
    q_lora, kv_lora = mla_q_norm.shape[1], mla_kv_norm.shape[1]
    nat_width = nat_rpb.shape[1] * NAT_HEAD_DIM
    mla_width = mla_out_norm.shape[1]
    assert mla_w_ukv.shape[2] == MLA_HEADS * 2 * LANES and mla_width == MLA_HEADS * LANES
    assert nat_width == mla_width and q_lora % LANES == 0 and kv_lora % LANES == 0
    t = batch * seq

    w1 = _prep_w_in(w_in, q_lora, kv_lora, nat_width)
    cq_col = 3 * nat_width // q_lora
    ckv_col = (3 * nat_width + q_lora) // kv_lora
    kpe_col = (3 * nat_width + q_lora + kv_lora) // LANES
    wq = _prep_w_uq(mla_w_uq)
    wkv = _prep_w_ukv(mla_w_ukv)
    wo = w_out.astype(BF16)
    wpq_t = jnp.swapaxes(peer_w_q, 1, 2).astype(BF16)
    keys = peer_sub_keys.astype(BF16)
    u = peer_u.astype(BF16)
    vt = jnp.swapaxes(peer_v, 1, 2).astype(BF16)
    bias_tbl = _nat_bias_tables(nat_rpb)
    cc, ss = _rope_tables(seq)

    xf = x.reshape(t, d)
    for l in range(depth):
        proj = _matmul([(xf, 0, d)], w1, l, out_dtype=BF16, gain=attn_norm[l], tn=2048, name="proj_in")
        q = _matmul([(proj, cq_col, q_lora)], wq, l, out_dtype=BF16, gain=mla_q_norm[l], tn=2048, name="mla_q_up")
        kv = _matmul([(proj, ckv_col, kv_lora)], wkv, l, out_dtype=BF16, gain=mla_kv_norm[l], tn=2048,
                     name="mla_kv_up")
        mla_o = _mla(q, kv, proj, cc, ss, mla_out_norm[l], batch=batch, seq=seq, kpe_col=kpe_col)
        nat_o = _nat(proj, bias_tbl, l, nat_out_norm[l], batch=batch, seq=seq)
        xf, ht = _matmul([(mla_o, 0, mla_width), (nat_o, 0, nat_width)], wo, l, out_dtype=F32, res=xf,
                         gain_out=ffn_norm[l], tn=d, name="mix_out")
        cnt, c1, r2, e2 = _route(ht, wpq_t, keys, l)
        xf = _experts(ht, u, vt, l, cnt, c1, r2, e2, xf, final_norm if l == depth - 1 else None)
    return xf.reshape(batch, seq, d)
```

```python
import functools

import numpy as np
import jax
import jax.numpy as jnp
from jax import lax
from jax.experimental import pallas as pl
from jax.experimental.pallas import tpu as pltpu

F32 = jnp.float32
BF16 = jnp.bfloat16

RMS_EPS = 1e-6
GRID_W = 64
MLA_HEADS = 8
MLA_NOPE = 128
MLA_ROPE = 64
ROPE_BASE = 10000.0
NAT_HEAD_DIM = 64
NAT_WIN_R = 8
NAT_WIN_C = 16
PEER_HEADS = 8
PEER_TOPK = 16
LANES = 128
SUB16 = 16
NEG_BIG = -1e30
MIB = 1024 * 1024


def _params(sem, vmem_mib):
    return pltpu.CompilerParams(dimension_semantics=sem, vmem_limit_bytes=vmem_mib * MIB)


def _rms(x, g):
    ms = jnp.mean(x * x, axis=-1, keepdims=True)
    return x * lax.rsqrt(ms + RMS_EPS) * g


def _rmsnorm_kernel(x_ref, g_ref, o_ref, *, transpose):
    y = _rms(x_ref[...].astype(F32), g_ref[...])
    if transpose:
        y = y.T
    o_ref[...] = y.astype(o_ref.dtype)


def _rmsnorm(x, g, *, out_dtype, transpose=False, tm=512):
    t, d = x.shape
    tm = min(tm, t)
    if transpose:
        out_shape = jax.ShapeDtypeStruct((d, t), out_dtype)
        out_spec = pl.BlockSpec((d, tm), lambda i: (0, i))
    else:
        out_shape = jax.ShapeDtypeStruct((t, d), out_dtype)
        out_spec = pl.BlockSpec((tm, d), lambda i: (i, 0))
    return pl.pallas_call(
        functools.partial(_rmsnorm_kernel, transpose=transpose),
        grid=(t // tm,),
        in_specs=[pl.BlockSpec((tm, d), lambda i: (i, 0)), pl.BlockSpec((1, d), lambda i: (0, 0))],
        out_specs=out_spec,
        out_shape=out_shape,
        compiler_params=_params(("parallel",), 40),
        name="rmsnorm_t" if transpose else "rmsnorm",
    )(x, g.reshape(1, d).astype(F32))


def _mm_kernel(*refs, n_a, norm, has_res, norm_out):
    a_refs = refs[:n_a]
    pos = n_a
    g_ref = refs[pos] if norm else None
    pos += int(norm)
    w_ref = refs[pos]
    pos += 1
    res_ref = refs[pos] if has_res else None
    pos += int(has_res)
    g2_ref = refs[pos] if norm_out else None
    pos += int(norm_out)
    o_ref = refs[pos]
    pos += 1
    ot_ref = refs[pos] if norm_out else None
    pos += int(norm_out)
    an_ref = refs[pos]

    @pl.when(pl.program_id(1) == 0)
    def _():
        parts = [r[...] for r in a_refs]
        a = parts[0] if n_a == 1 else jnp.concatenate(parts, axis=1)
        if norm:
            a = _rms(a.astype(F32), g_ref[...])
        an_ref[...] = a.astype(BF16)

    acc = jnp.dot(an_ref[...], w_ref[...], preferred_element_type=F32)
    if has_res:
        acc = acc + res_ref[...]
    o_ref[...] = acc.astype(o_ref.dtype)
    if norm_out:
        ot_ref[...] = _rms(acc, g2_ref[...]).T.astype(ot_ref.dtype)


def _matmul(a_list, w, layer, *, out_dtype, gain=None, res=None, gain_out=None, tm=512, tn=512, name="matmul"):
    t = a_list[0][0].shape[0]
    _, k, n = w.shape
    assert k == sum(width for _, _, width in a_list)
    tm, tn = min(tm, t), min(tn, n)
    in_specs, args = [], []
    for arr, cb, width in a_list:
        in_specs.append(pl.BlockSpec((tm, width), lambda i, j, cb=cb: (i, cb)))
        args.append(arr)
    if gain is not None:
        in_specs.append(pl.BlockSpec((1, k), lambda i, j: (0, 0)))
        args.append(gain.reshape(1, k).astype(F32))
    in_specs.append(pl.BlockSpec((None, k, tn), lambda i, j: (layer, 0, j)))
    args.append(w)
    if res is not None:
        in_specs.append(pl.BlockSpec((tm, tn), lambda i, j: (i, j)))
        args.append(res)
    out_specs = pl.BlockSpec((tm, tn), lambda i, j: (i, j))
    out_shape = jax.ShapeDtypeStruct((t, n), out_dtype)
    if gain_out is not None:
        assert tn == n
        in_specs.append(pl.BlockSpec((1, n), lambda i, j: (0, 0)))
        args.append(gain_out.reshape(1, n).astype(F32))
        out_specs = [out_specs, pl.BlockSpec((n, tm), lambda i, j: (0, i))]
        out_shape = [out_shape, jax.ShapeDtypeStruct((n, t), BF16)]
    return pl.pallas_call(
        functools.partial(_mm_kernel, n_a=len(a_list), norm=gain is not None, has_res=res is not None,
                          norm_out=gain_out is not None),
        grid=(t // tm, n // tn),
        in_specs=in_specs,
        out_specs=out_specs,
        out_shape=out_shape,
        scratch_shapes=[pltpu.VMEM((tm, k), BF16)],
        compiler_params=_params(("parallel", "arbitrary"), 48),
        name=name,
    )(*args)


def _mla_kernel(q_ref, k_ref, v_ref, ka_ref, kb_ref, cck_ref, ssk_ref, ccq_ref, ssq_ref, g_ref, o_ref, kcat_ref,
                s_ref, p_ref, *, heads, scale):
    @pl.when(pl.program_id(1) == 0)
    def _():
        rk = ka_ref[...].astype(F32) * cck_ref[...] + kb_ref[...].astype(F32) * ssk_ref[...]
        rk = rk.astype(BF16)
        for h in range(heads):
            kcat_ref[h, :, 0:LANES] = k_ref[:, h * LANES:(h + 1) * LANES]
            kcat_ref[h, :, LANES:2 * LANES] = rk

    tq = q_ref.shape[0]
    pe0 = heads * MLA_NOPE
    sw0 = pe0 + (heads // 2) * LANES
    ccq, ssq = ccq_ref[...], ssq_ref[...]
    lane = lax.broadcasted_iota(jnp.int32, (tq, LANES), 1)

    def scores(h):
        p_lo = pe0 + (h // 2) * LANES
        s_lo = sw0 + (h // 2) * LANES
        rq = q_ref[:, p_lo:p_lo + LANES].astype(F32) * ccq + q_ref[:, s_lo:s_lo + LANES].astype(F32) * ssq
        keep = (lane < MLA_ROPE) if h % 2 == 0 else (lane >= MLA_ROPE)
        rq = jnp.where(keep, rq, 0.0).astype(BF16)
        qcat = jnp.concatenate([q_ref[:, h * LANES:(h + 1) * LANES], rq], axis=1)
        s_ref[h % 2] = lax.dot_general(qcat, kcat_ref[h], (((1,), (1,)), ((), ())), preferred_element_type=F32)

    def softmax(h):
        c = scale * 1.4426950408889634
        inv = []
        for r in range(tq // SUB16):
            rows = slice(r * SUB16, (r + 1) * SUB16)
            s = s_ref[h % 2, rows, :]
            m = jnp.max(s, axis=-1, keepdims=True)
            p = jnp.exp2((s - m) * c)
            inv.append(1.0 / jnp.sum(p, axis=-1, keepdims=True))
            p_ref[h % 2, rows, :] = p.astype(BF16)
        return jnp.concatenate(inv, axis=0)

    def values(h, inv_l):
        return jnp.dot(p_ref[h % 2], v_ref[:, h * LANES:(h + 1) * LANES], preferred_element_type=F32) * inv_l

    outs, inv_l = [], {}
    scores(0)
    for h in range(heads):
        if h + 1 < heads:
            scores(h + 1)
        inv_l[h] = softmax(h)
        if h >= 1:
            outs.append(values(h - 1, inv_l[h - 1]))
    outs.append(values(heads - 1, inv_l[heads - 1]))
    o_all = jnp.concatenate(outs, axis=1)
    o_ref[...] = _rms(o_all, g_ref[...]).astype(o_ref.dtype)


def _mla(q, kv, proj, cc, ss, gain, *, batch, seq, kpe_col, tq=512):
    heads = MLA_HEADS
    t = q.shape[0]
    tq = min(tq, seq)
    nq = seq // tq
    width = heads * LANES
    return pl.pallas_call(
        functools.partial(_mla_kernel, heads=heads, scale=float((MLA_NOPE + MLA_ROPE) ** -0.5)),
        grid=(batch, nq),
        in_specs=[
            pl.BlockSpec((tq, q.shape[1]), lambda b, i: (b * nq + i, 0)),
            pl.BlockSpec((seq, width), lambda b, i: (b, 0)),
            pl.BlockSpec((seq, width), lambda b, i: (b, 1)),
            pl.BlockSpec((seq, LANES), lambda b, i: (b, kpe_col)),
            pl.BlockSpec((seq, LANES), lambda b, i: (b, kpe_col + 1)),
            pl.BlockSpec((seq, LANES), lambda b, i: (0, 0)),
            pl.BlockSpec((seq, LANES), lambda b, i: (0, 0)),
            pl.BlockSpec((tq, LANES), lambda b, i: (i, 0)),
            pl.BlockSpec((tq, LANES), lambda b, i: (i, 0)),
            pl.BlockSpec((1, width), lambda b, i: (0, 0)),
        ],
        out_specs=pl.BlockSpec((tq, width), lambda b, i: (b * nq + i, 0)),
        out_shape=jax.ShapeDtypeStruct((t, width), BF16),
        scratch_shapes=[
            pltpu.VMEM((heads, seq, 2 * LANES), BF16),
            pltpu.VMEM((2, tq, seq), F32),
            pltpu.VMEM((2, tq, seq), BF16),
        ],
        compiler_params=_params(("parallel", "arbitrary"), 56),
        name="mla_attention",
    )(q, kv, kv, proj, proj, cc, ss, cc, ss, gain.reshape(1, width).astype(F32))


NAT_ROWS_PER_STEP = 2


def _nat_kernel(q_ref, k_ref, v_ref, bias_ref, g_ref, o_ref, s_ref, p_ref, *, rows, pairs):
    band = NAT_WIN_R * GRID_W
    lane = lax.broadcasted_iota(jnp.int32, (GRID_W, LANES), 1)
    first = lane < NAT_HEAD_DIM
    for rr in range(NAT_ROWS_PER_STEP):
        r = pl.program_id(1) * NAT_ROWS_PER_STEP + rr
        qrows = slice(rr * GRID_W, (rr + 1) * GRID_W)
        rs = jnp.clip(r - NAT_WIN_R // 2, 0, rows - NAT_WIN_R)
        start = rs - r + (NAT_WIN_R - 1)
        koff = pl.multiple_of(rs * GRID_W, GRID_W)
        for j in range(pairs):
            cols = slice(j * LANES, (j + 1) * LANES)
            q2 = q_ref[qrows, cols] * jnp.asarray(NAT_HEAD_DIM ** -0.5, BF16)
            zero = jnp.zeros_like(q2)
            qq = jnp.concatenate([jnp.where(first, q2, zero), jnp.where(first, zero, q2)], axis=0)
            kb = k_ref[pl.ds(koff, band), cols]
            s_ref[rr, j] = lax.dot_general(qq, kb, (((1,), (1,)), ((), ())), preferred_element_type=F32)
        inv_l = []
        for j in range(pairs):
            bias = jnp.concatenate([bias_ref[j, start + 2 * m] for m in range(NAT_WIN_R // 2)], axis=1)
            s = s_ref[rr, j] + bias
            m = jnp.max(s, axis=-1, keepdims=True)
            p = jnp.exp(s - m)
            inv_l.append(1.0 / jnp.sum(p, axis=-1, keepdims=True))
            p_ref[rr, j] = p.astype(BF16)
        outs = []
        for j in range(pairs):
            cols = slice(j * LANES, (j + 1) * LANES)
            vb = v_ref[pl.ds(koff, band), cols]
            o = jnp.dot(p_ref[rr, j], vb, preferred_element_type=F32) * inv_l[j]
            outs.append(jnp.where(first, o[:GRID_W], o[GRID_W:]))
        o_all = jnp.concatenate(outs, axis=1)
        o_ref[qrows, :] = _rms(o_all, g_ref[...]).astype(o_ref.dtype)


def _nat(proj, bias_tbl, layer, gain, *, batch, seq):
    rows = seq // GRID_W
    assert rows >= NAT_WIN_R and rows % NAT_ROWS_PER_STEP == 0
    t = proj.shape[0]
    pairs, n_off = bias_tbl.shape[1], bias_tbl.shape[2]
    width = pairs * LANES
    steps = rows // NAT_ROWS_PER_STEP
    q_rows = NAT_ROWS_PER_STEP * GRID_W
    return pl.pallas_call(
        functools.partial(_nat_kernel, rows=rows, pairs=pairs),
        grid=(batch, steps),
        in_specs=[
            pl.BlockSpec((q_rows, width), lambda b, r: (b * steps + r, 0)),
            pl.BlockSpec((seq, width), lambda b, r: (b, 1)),
            pl.BlockSpec((seq, width), lambda b, r: (b, 2)),
            pl.BlockSpec((None, pairs, n_off, 2 * GRID_W, 2 * GRID_W), lambda b, r: (layer, 0, 0, 0, 0)),
            pl.BlockSpec((1, width), lambda b, r: (0, 0)),
        ],
        out_specs=pl.BlockSpec((q_rows, width), lambda b, r: (b * steps + r, 0)),
        out_shape=jax.ShapeDtypeStruct((t, width), BF16),
        scratch_shapes=[
            pltpu.VMEM((NAT_ROWS_PER_STEP, pairs, 2 * GRID_W, NAT_WIN_R * GRID_W), F32),
            pltpu.VMEM((NAT_ROWS_PER_STEP, pairs, 2 * GRID_W, NAT_WIN_R * GRID_W), BF16),
        ],
        compiler_params=_params(("parallel", "arbitrary"), 48),
        name="nat_attention",
    )(proj, proj, proj, bias_tbl, gain.reshape(1, width).astype(F32))


def _nat_bias_tables(rpb):
    n_l, heads, n_off = rpb.shape[0], rpb.shape[1], rpb.shape[2]
    qc = np.arange(GRID_W)[:, None]
    kc = np.arange(GRID_W)[None, :]
    c_start = np.clip(qc - NAT_WIN_C // 2, 0, GRID_W - NAT_WIN_C)
    valid = (kc >= c_start) & (kc < c_start + NAT_WIN_C)
    dc_idx = np.clip(kc - qc, -(NAT_WIN_C - 1), NAT_WIN_C - 1) + (NAT_WIN_C - 1)
    tbl = rpb.astype(F32)[:, :, :, dc_idx]
    tbl = jnp.where(jnp.asarray(valid), tbl, NEG_BIG)
    tbl = tbl.reshape(n_l, heads // 2, 2, n_off, GRID_W, GRID_W)
    tbl = jnp.transpose(tbl, (0, 1, 3, 2, 4, 5)).reshape(n_l, heads // 2, n_off, 2 * GRID_W, GRID_W)
    return jnp.concatenate([tbl[:, :, :-1], tbl[:, :, 1:]], axis=-1)


def _extract(x, rows, exact):
    m = jnp.max(x, axis=0, keepdims=True)
    hit = x == m
    if exact:
        first = jnp.min(jnp.where(hit, rows, float(x.shape[0])), axis=0, keepdims=True)
        hit = rows == first
    return m, hit


def _row_index(shape):
    return lax.broadcasted_iota(jnp.int32, shape, 0).astype(F32)


def _top_rows_ranked(x, dst_ref, k, exact):
    rows = _row_index(x.shape) if exact else None
    rank = jnp.full(x.shape, float(k), F32)
    for i in range(k):
        m, hit = _extract(x, rows, exact)
        dst_ref[i:i + 1, :] = m
        rank = jnp.where(hit, float(i), rank)
        x = jnp.where(hit, -jnp.inf, x)
    return rank, x


def _top2_rows(x1, x2, dst1_ref, dst2_ref, k, exact):
    rows = _row_index(x1.shape) if exact else None
    rank1 = jnp.full(x1.shape, float(k), F32) if exact else None
    rank2 = jnp.full(x2.shape, float(k), F32)
    for i in range(k):
        m1, hit1 = _extract(x1, rows, exact)
        m2, hit2 = _extract(x2, rows, exact)
        dst1_ref[i:i + 1, :] = m1
        dst2_ref[i:i + 1, :] = m2
        if exact:
            rank1 = jnp.where(hit1, float(i), rank1)
        x1 = jnp.where(hit1, -jnp.inf, x1)
        rank2 = jnp.where(hit2, float(i), rank2)
        x2 = jnp.where(hit2, -jnp.inf, x2)
    return rank1, rank2, x1, x2


def _count_removed(x):
    return jnp.sum(jnp.where(x == -jnp.inf, 1.0, 0.0), axis=0, keepdims=True)


def _dup_bf16_bits(x):
    bits = pltpu.bitcast(x.astype(BF16).astype(F32), jnp.uint32)
    return bits | (bits >> 16)


def _route_group(s1, s2, a_ref, b_ref, cand_ref, top_ref, exact):
    nk = s1.shape[0]
    rank1, rank2, x1, x2 = _top2_rows(s1, s2, a_ref, b_ref, PEER_TOPK, exact)
    offs, off = [], 0
    for r in range(PEER_TOPK):
        n_r = PEER_TOPK // (r + 1)
        cand_ref[off:off + n_r, :] = a_ref[r:r + 1, :] + b_ref[0:n_r, :]
        offs.append((off, n_r))
        off += n_r
    n_pad = cand_ref.shape[0] - off
    if n_pad:
        cand_ref[off:, :] = jnp.full((n_pad, LANES), -jnp.inf, F32)
    crank, xc = _top_rows_ranked(cand_ref[...], top_ref, PEER_TOPK, exact)
    taken = jnp.where(crank < float(PEER_TOPK), 1.0, 0.0)
    top = top_ref[...]
    z = jnp.sum(jnp.exp(top - top[0:1]), axis=0, keepdims=True)
    cnt = jnp.zeros((nk, LANES), F32)
    for r, (off, n_r) in enumerate(offs):
        cnt_r = jnp.sum(taken[off:off + n_r], axis=0, keepdims=True)
        is_r = (rank1 == float(r)) if exact else (s1 == a_ref[r:r + 1, :])
        cnt = jnp.where(is_r, cnt_r, cnt)
    c1 = jnp.exp(s1 - a_ref[0:1, :]) / z
    e2 = jnp.exp(s2 - b_ref[0:1, :])
    ties = None
    if not exact:
        k = float(PEER_TOPK)
        ties = ((_count_removed(x1) > k) | (_count_removed(x2) > k) | (_count_removed(xc) > k + n_pad))
    return cnt, c1, rank2, e2, ties


def _route_kernel(ht_ref, wq_ref, keys_ref, cnt_ref, c1_ref, r2_ref, e2_ref, a_ref, b_ref, cand_ref, top_ref):
    tm = ht_ref.shape[1]
    qt = jnp.dot(wq_ref[...], ht_ref[...], preferred_element_type=F32)
    half = qt.shape[0] // 2
    s1_all = jnp.dot(keys_ref[0], qt[:half].astype(BF16), preferred_element_type=F32)
    s2_all = jnp.dot(keys_ref[1], qt[half:].astype(BF16), preferred_element_type=F32)

    def store(c, cnt, c1, rank2, e2):
        lanes = slice(c * LANES, (c + 1) * LANES)
        cnt_ref[0, :, lanes] = _dup_bf16_bits(cnt)
        c1_ref[0, :, lanes] = _dup_bf16_bits(c1)
        r2_ref[0, :, lanes] = rank2.astype(BF16)
        e2_ref[0, :, lanes] = e2.astype(BF16)

    def group(c, exact):
        lanes = slice(c * LANES, (c + 1) * LANES)
        return _route_group(s1_all[:, lanes], s2_all[:, lanes], a_ref, b_ref, cand_ref, top_ref, exact)

    tie = []
    for c in range(tm // LANES):
        cnt, c1, rank2, e2, ties = group(c, exact=False)
        store(c, cnt, c1, rank2, e2)
        tie.append(jnp.where(ties, 1.0, 0.0))

    @pl.when(jnp.max(functools.reduce(jnp.maximum, tie)) > 0.0)
    def _():
        for c in range(tm // LANES):
            @pl.when(jnp.max(tie[c]) > 0.0)
            def _(c=c):
                store(c, *group(c, exact=True)[:4])


def _route(ht, wqt, keys, layer, *, tm=1024):
    d, t = ht.shape
    heads = PEER_HEADS
    dq2 = wqt.shape[1] // heads
    nk = keys.shape[2]
    tm = min(tm, t)
    n_cand = -(-sum(PEER_TOPK // (r + 1) for r in range(PEER_TOPK)) // 8) * 8
    row_shape = jax.ShapeDtypeStruct((heads, nk, t), jnp.uint32)
    row_spec = pl.BlockSpec((1, nk, tm), lambda i, h: (h, 0, i))
    tile_shape = jax.ShapeDtypeStruct((heads, nk, t), BF16)
    tile_spec = pl.BlockSpec((1, nk, tm), lambda i, h: (h, 0, i))
    return pl.pallas_call(
        _route_kernel,
        grid=(t // tm, heads),
        in_specs=[
            pl.BlockSpec((d, tm), lambda i, h: (0, i)),
            pl.BlockSpec((None, dq2, d), lambda i, h: (layer, h, 0)),
            pl.BlockSpec((None,) + keys.shape[1:], lambda i, h: (layer, 0, 0, 0)),
        ],
        out_specs=[row_spec, row_spec, tile_spec, tile_spec],
        out_shape=[row_shape, row_shape, tile_shape, tile_shape],
        scratch_shapes=[
            pltpu.VMEM((PEER_TOPK, LANES), F32),
            pltpu.VMEM((PEER_TOPK, LANES), F32),
            pltpu.VMEM((n_cand, LANES), F32),
            pltpu.VMEM((PEER_TOPK, LANES), F32),
        ],
        compiler_params=_params(("parallel", "arbitrary"), 40),
        name="peer_route",
    )(ht, wqt, keys)


def _gelu(x):
    return 0.5 * x * (1.0 + lax.erf(x * 0.7071067811865476))


GATE_LANES = 256
GATE_KGROUP = 2


def _row_as_bf16(row):
    return pltpu.bitcast(jnp.broadcast_to(row, (8, row.shape[1])), BF16)


def _build_gates(r2_ref, e2_ref, cnt_ref, c1_ref, row0, put, *, n1, heads):
    nk = r2_ref.shape[1]
    tm = r2_ref.shape[2]
    n_sub = nk // SUB16
    for k0 in range(0, n1, GATE_KGROUP):
        ks = range(k0, min(k0 + GATE_KGROUP, n1))
        for c in range(tm // GATE_LANES):
            lanes = slice(c * GATE_LANES, (c + 1) * GATE_LANES)
            g = {(k, s): jnp.zeros((SUB16, GATE_LANES), BF16) for k in ks for s in range(n_sub)}
            for h in range(heads):
                cnt = {k: _row_as_bf16(cnt_ref[h, row0 + k:row0 + k + 1, lanes]) for k in ks}
                c1 = {k: _row_as_bf16(c1_ref[h, row0 + k:row0 + k + 1, lanes]) for k in ks}
                for s in range(n_sub):
                    rows = slice(s * SUB16, (s + 1) * SUB16)
                    r2 = r2_ref[h, rows, lanes]
                    e2 = e2_ref[h, rows, lanes]
                    for k in ks:
                        g[k, s] = g[k, s] + jnp.where(r2 < cnt[k], e2, jnp.zeros_like(e2)) * c1[k]
            for k in ks:
                for s in range(n_sub):
                    put(k, s, c, g[k, s])


def _experts_kernel(ht_ref, u_ref, vt_ref, cnt_ref, c1_ref, cntn_ref, c1n_ref, r2_ref, e2_ref, x_ref, gf_ref,
                    o_ref, acc_ref, ga_ref, a_ref, *, n1, heads, final_norm):
    j = pl.program_id(1)
    nk = r2_ref.shape[1]
    tm = ht_ref.shape[1]
    te = n1 * nk
    n_sub = nk // SUB16
    build = functools.partial(_build_gates, r2_ref, e2_ref, n1=n1, heads=heads)
    slot, next_slot = j % 2, (j + 1) % 2

    def put_slot(which):
        def put(k, s, c, tile):
            ga_ref[which, k * nk + s * SUB16:k * nk + (s + 1) * SUB16, c * GATE_LANES:(c + 1) * GATE_LANES] = tile
        return put

    @pl.when(j == 0)
    def _():
        acc_ref[...] = jnp.zeros_like(acc_ref)
        build(cnt_ref, c1_ref, 0, put_slot(0))

    act0 = jnp.dot(u_ref[0:te, :], ht_ref[...], preferred_element_type=F32)
    a_ref[0:te, :] = _gelu(act0).astype(BF16) * ga_ref[slot]
    build(cntn_ref, c1n_ref, 0, put_slot(next_slot))
    tiles = {}
    build(cnt_ref, c1_ref, n1, lambda k, s, c, tile: tiles.__setitem__((k, s, c), tile))
    gb = jnp.concatenate(
        [jnp.concatenate([tiles[k, s, c] for c in range(tm // GATE_LANES)], axis=1)
         for k in range(n1) for s in range(n_sub)], axis=0)
    act1 = jnp.dot(u_ref[te:2 * te, :], ht_ref[...], preferred_element_type=F32)
    a_ref[te:2 * te, :] = _gelu(act1).astype(BF16) * gb
    acc_ref[...] += jnp.dot(vt_ref[...], a_ref[...], preferred_element_type=F32)

    @pl.when(j == pl.num_programs(1) - 1)
    def _():
        y = x_ref[...] + acc_ref[...].T
        o_ref[...] = _rms(y, gf_ref[...]) if final_norm else y


def _experts(ht, u, vt, layer, cnt, c1, r2, e2, x, gain_final=None, *, tm=512, n1=4):
    d, t = ht.shape
    heads, nk, _ = cnt.shape
    n_e = u.shape[1]
    tm = min(tm, t)
    te = n1 * nk
    n_j = n_e // (2 * te)
    assert n_e == n_j * 2 * te and (2 * n1) % 8 == 0
    rows_now = pl.BlockSpec((heads, 2 * n1, tm), lambda i, j: (0, j, i))
    rows_next = pl.BlockSpec((heads, 2 * n1, tm), lambda i, j: (0, jnp.minimum(j + 1, n_j - 1), i))
    tile_spec = pl.BlockSpec((heads, nk, tm), lambda i, j: (0, 0, i))
    return pl.pallas_call(
        functools.partial(_experts_kernel, n1=n1, heads=heads, final_norm=gain_final is not None),
        grid=(t // tm, n_j),
        in_specs=[
            pl.BlockSpec((d, tm), lambda i, j: (0, i)),
            pl.BlockSpec((None, 2 * te, d), lambda i, j: (layer, j, 0)),
            pl.BlockSpec((None, d, 2 * te), lambda i, j: (layer, 0, j)),
            rows_now, rows_now, rows_next, rows_next, tile_spec, tile_spec,
            pl.BlockSpec((tm, d), lambda i, j: (i, 0)),
            pl.BlockSpec((1, d), lambda i, j: (0, 0)),
        ],
        out_specs=pl.BlockSpec((tm, d), lambda i, j: (i, 0)),
        out_shape=jax.ShapeDtypeStruct((t, d), F32),
        scratch_shapes=[
            pltpu.VMEM((d, tm), F32),
            pltpu.VMEM((2, te, tm), BF16),
            pltpu.VMEM((2 * te, tm), BF16),
        ],
        compiler_params=_params(("parallel", "arbitrary"), 56),
        name="peer_experts",
    )(ht, u, vt, cnt, c1, cnt, c1, r2, e2, x,
      (jnp.ones((d,), F32) if gain_final is None else gain_final).reshape(1, d).astype(F32))


def _swap_halves(w):
    half = w.shape[-1] // 2
    return jnp.concatenate([w[..., half:], w[..., :half]], axis=-1)


def _prep_w_in(w_in, q_lora, kv_lora, nat_width):
    o1 = q_lora
    o2 = o1 + kv_lora
    o3 = o2 + MLA_ROPE
    w_in = w_in.astype(BF16)
    c_q, c_kv, k_pe, nat = w_in[..., :o1], w_in[..., o1:o2], w_in[..., o2:o3], w_in[..., o3:]
    assert nat.shape[-1] == 3 * nat_width
    k_sw = _swap_halves(k_pe)
    return jnp.concatenate([nat, c_q, c_kv, k_pe, k_pe, k_sw, k_sw], axis=-1)


def _prep_w_uq(w_uq):
    n_l, k, _ = w_uq.shape
    w = w_uq.reshape(n_l, k, MLA_HEADS, MLA_NOPE + MLA_ROPE)
    nope = w[..., :MLA_NOPE].reshape(n_l, k, MLA_HEADS * MLA_NOPE)
    pe = w[..., MLA_NOPE:]
    return jnp.concatenate([nope, pe.reshape(n_l, k, -1), _swap_halves(pe).reshape(n_l, k, -1)], axis=-1).astype(BF16)


def _prep_w_ukv(w_ukv):
    n_l, k, n = w_ukv.shape
    w = w_ukv.reshape(n_l, k, MLA_HEADS, n // MLA_HEADS)
    return jnp.concatenate([w[..., :MLA_NOPE].reshape(n_l, k, -1), w[..., MLA_NOPE:].reshape(n_l, k, -1)],
                           axis=-1).astype(BF16)


def _rope_tables(seq):
    inv = ROPE_BASE ** (-jnp.arange(0, MLA_ROPE, 2, dtype=F32) / MLA_ROPE)
    ang = jnp.arange(seq, dtype=F32)[:, None] * inv[None, :]
    cos, sin = jnp.cos(ang), jnp.sin(ang)
    return jnp.concatenate([cos, cos, cos, cos], axis=-1), jnp.concatenate([-sin, sin, -sin, sin], axis=-1)


def kernel(x, attn_norm, w_in, mla_q_norm, mla_w_uq, mla_kv_norm, mla_w_ukv, nat_rpb, mla_out_norm, nat_out_norm,
           w_out, ffn_norm, peer_w_q, peer_sub_keys, peer_u, peer_v, final_norm):
    batch, seq, d = x.shape
    depth = w_in.shape[0]
    q_lora, kv_lora = mla_q_norm.shape[1], mla_kv_norm.shape[1]
    nat_width = nat_rpb.shape[1] * NAT_HEAD_DIM
    mla_width = mla_out_norm.shape[1]
    assert mla_w_ukv.shape[2] == MLA_HEADS * 2 * LANES and mla_width == MLA_HEADS * LANES
    assert nat_width == mla_width and q_lora % LANES == 0 and kv_lora % LANES == 0
    t = batch * seq

    w1 = _prep_w_in(w_in, q_lora, kv_lora, nat_width)
    cq_col = 3 * nat_width // q_lora
    ckv_col = (3 * nat_width + q_lora) // kv_lora
    kpe_col = (3 * nat_width + q_lora + kv_lora) // LANES
    wq = _prep_w_uq(mla_w_uq)
    wkv = _prep_w_ukv(mla_w_ukv)
    wo = w_out.astype(BF16)
    wpq_t = jnp.swapaxes(peer_w_q, 1, 2).astype(BF16)
    keys = peer_sub_keys.astype(BF16)
    u = peer_u.astype(BF16)
    vt = jnp.swapaxes(peer_v, 1, 2).astype(BF16)
    bias_tbl = _nat_bias_tables(nat_rpb)
    cc, ss = _rope_tables(seq)

    xf = x.reshape(t, d)
    for l in range(depth):
        proj = _matmul([(xf, 0, d)], w1, l, out_dtype=BF16, gain=attn_norm[l], tn=2048, name="proj_in")
        q = _matmul([(proj, cq_col, q_lora)], wq, l, out_dtype=BF16, gain=mla_q_norm[l], tn=2048, name="mla_q_up")
        kv = _matmul([(proj, ckv_col, kv_lora)], wkv, l, out_dtype=BF16, gain=mla_kv_norm[l], tn=2048,
                     name="mla_kv_up")
        mla_o = _mla(q, kv, proj, cc, ss, mla_out_norm[l], batch=batch, seq=seq, kpe_col=kpe_col)
        nat_o = _nat(proj, bias_tbl, l, nat_out_norm[l], batch=batch, seq=seq)
        xf, ht = _matmul([(mla_o, 0, mla_width), (nat_o, 0, nat_width)], wo, l, out_dtype=F32, res=xf,
                         gain_out=ffn_norm[l], tn=d, name="mix_out")
        cnt, c1, r2, e2 = _route(ht, wpq_t, keys, l)
        xf = _experts(ht, u, vt, l, cnt, c1, r2, e2, xf, final_norm if l == depth - 1 else None)
    return xf.reshape(batch, seq, d)
```

```python
import functools

import numpy as np
import jax
import jax.numpy as jnp
from jax import lax
from jax.experimental import pallas as pl
from jax.experimental.pallas import tpu as pltpu

F32 = jnp.float32
BF16 = jnp.bfloat16

RMS_EPS = 1e-6
GRID_W = 64
MLA_HEADS = 8
MLA_NOPE = 128
MLA_ROPE = 64
ROPE_BASE = 10000.0
NAT_HEAD_DIM = 64
NAT_WIN_R = 8
NAT_WIN_C = 16
PEER_HEADS = 8
PEER_TOPK = 16
LANES = 128
SUB16 = 16
NEG_BIG = -1e30
MIB = 1024 * 1024


def _params(sem, vmem_mib):
    return pltpu.CompilerParams(dimension_semantics=sem, vmem_limit_bytes=vmem_mib * MIB)


def _rms(x, g):
    ms = jnp.mean(x * x, axis=-1, keepdims=True)
    return x * lax.rsqrt(ms + RMS_EPS) * g


def _rmsnorm_kernel(x_ref, g_ref, o_ref, *, transpose):
    y = _rms(x_ref[...].astype(F32), g_ref[...])
    if transpose:
        y = y.T
    o_ref[...] = y.astype(o_ref.dtype)


def _rmsnorm(x, g, *, out_dtype, transpose=False, tm=512):
    t, d = x.shape
    tm = min(tm, t)
    if transpose:
        out_shape = jax.ShapeDtypeStruct((d, t), out_dtype)
        out_spec = pl.BlockSpec((d, tm), lambda i: (0, i))
    else:
        out_shape = jax.ShapeDtypeStruct((t, d), out_dtype)
        out_spec = pl.BlockSpec((tm, d), lambda i: (i, 0))
    return pl.pallas_call(
        functools.partial(_rmsnorm_kernel, transpose=transpose),
        grid=(t // tm,),
        in_specs=[pl.BlockSpec((tm, d), lambda i: (i, 0)), pl.BlockSpec((1, d), lambda i: (0, 0))],
        out_specs=out_spec,
        out_shape=out_shape,
        compiler_params=_params(("parallel",), 40),
        name="rmsnorm_t" if transpose else "rmsnorm",
    )(x, g.reshape(1, d).astype(F32))


def _mm_kernel(*refs, n_a, norm, has_res, norm_out):
    a_refs = refs[:n_a]
    pos = n_a
    g_ref = refs[pos] if norm else None
    pos += int(norm)
    w_ref = refs[pos]
    pos += 1
    res_ref = refs[pos] if has_res else None
    pos += int(has_res)
    g2_ref = refs[pos] if norm_out else None
    pos += int(norm_out)
    o_ref = refs[pos]
    pos += 1
    ot_ref = refs[pos] if norm_out else None
    pos += int(norm_out)
    an_ref = refs[pos]

    @pl.when(pl.program_id(1) == 0)
    def _():
        parts = [r[...] for r in a_refs]
        a = parts[0] if n_a == 1 else jnp.concatenate(parts, axis=1)
        if norm:
            a = _rms(a.astype(F32), g_ref[...])
        an_ref[...] = a.astype(BF16)

    acc = jnp.dot(an_ref[...], w_ref[...], preferred_element_type=F32)
    if has_res:
        acc = acc + res_ref[...]
    o_ref[...] = acc.astype(o_ref.dtype)
    if norm_out:
        ot_ref[...] = _rms(acc, g2_ref[...]).T.astype(ot_ref.dtype)


def _matmul(a_list, w, layer, *, out_dtype, gain=None, res=None, gain_out=None, tm=512, tn=512, name="matmul"):
    t = a_list[0][0].shape[0]
    _, k, n = w.shape
    assert k == sum(width for _, _, width in a_list)
    tm, tn = min(tm, t), min(tn, n)
    in_specs, args = [], []
    for arr, cb, width in a_list:
        in_specs.append(pl.BlockSpec((tm, width), lambda i, j, cb=cb: (i, cb)))
        args.append(arr)
    if gain is not None:
        in_specs.append(pl.BlockSpec((1, k), lambda i, j: (0, 0)))
        args.append(gain.reshape(1, k).astype(F32))
    in_specs.append(pl.BlockSpec((None, k, tn), lambda i, j: (layer, 0, j)))
    args.append(w)
    if res is not None:
        in_specs.append(pl.BlockSpec((tm, tn), lambda i, j: (i, j)))
        args.append(res)
    out_specs = pl.BlockSpec((tm, tn), lambda i, j: (i, j))
    out_shape = jax.ShapeDtypeStruct((t, n), out_dtype)
    if gain_out is not None:
        assert tn == n
        in_specs.append(pl.BlockSpec((1, n), lambda i, j: (0, 0)))
        args.append(gain_out.reshape(1, n).astype(F32))
        out_specs = [out_specs, pl.BlockSpec((n, tm), lambda i, j: (0, i))]
        out_shape = [out_shape, jax.ShapeDtypeStruct((n, t), BF16)]
    return pl.pallas_call(
        functools.partial(_mm_kernel, n_a=len(a_list), norm=gain is not None, has_res=res is not None,
                          norm_out=gain_out is not None),
        grid=(t // tm, n // tn),
        in_specs=in_specs,
        out_specs=out_specs,
        out_shape=out_shape,
        scratch_shapes=[pltpu.VMEM((tm, k), BF16)],
        compiler_params=_params(("parallel", "arbitrary"), 48),
        name=name,
    )(*args)


def _mla_kernel(q_ref, k_ref, v_ref, ka_ref, kb_ref, cck_ref, ssk_ref, ccq_ref, ssq_ref, g_ref, o_ref, kcat_ref,
                s_ref, p_ref, *, heads, scale):
    @pl.when(pl.program_id(1) == 0)
    def _():
        rk = ka_ref[...].astype(F32) * cck_ref[...] + kb_ref[...].astype(F32) * ssk_ref[...]
        rk = rk.astype(BF16)
        for h in range(heads):
            kcat_ref[h, :, 0:LANES] = k_ref[:, h * LANES:(h + 1) * LANES]
            kcat_ref[h, :, LANES:2 * LANES] = rk

    tq = q_ref.shape[0]
    pe0 = heads * MLA_NOPE
    sw0 = pe0 + (heads // 2) * LANES
    ccq, ssq = ccq_ref[...], ssq_ref[...]
    lane = lax.broadcasted_iota(jnp.int32, (tq, LANES), 1)

    def scores(h):
        p_lo = pe0 + (h // 2) * LANES
        s_lo = sw0 + (h // 2) * LANES
        rq = q_ref[:, p_lo:p_lo + LANES].astype(F32) * ccq + q_ref[:, s_lo:s_lo + LANES].astype(F32) * ssq
        keep = (lane < MLA_ROPE) if h % 2 == 0 else (lane >= MLA_ROPE)
        rq = jnp.where(keep, rq, 0.0).astype(BF16)
        qcat = jnp.concatenate([q_ref[:, h * LANES:(h + 1) * LANES], rq], axis=1)
        s_ref[h % 2] = lax.dot_general(qcat, kcat_ref[h], (((1,), (1,)), ((), ())), preferred_element_type=F32)

    def softmax(h):
        c = scale * 1.4426950408889634
        inv = []
        for r in range(tq // SUB16):
            rows = slice(r * SUB16, (r + 1) * SUB16)
            s = s_ref[h % 2, rows, :]
            m = jnp.max(s, axis=-1, keepdims=True)
            p = jnp.exp2((s - m) * c)
            inv.append(1.0 / jnp.sum(p, axis=-1, keepdims=True))
            p_ref[h % 2, rows, :] = p.astype(BF16)
        return jnp.concatenate(inv, axis=0)

    def values(h, inv_l):
        return jnp.dot(p_ref[h % 2], v_ref[:, h * LANES:(h + 1) * LANES], preferred_element_type=F32) * inv_l

    outs, inv_l = [], {}
    scores(0)
    for h in range(heads):
        if h + 1 < heads:
            scores(h + 1)
        inv_l[h] = softmax(h)
        if h >= 1:
            outs.append(values(h - 1, inv_l[h - 1]))
    outs.append(values(heads - 1, inv_l[heads - 1]))
    o_all = jnp.concatenate(outs, axis=1)
    o_ref[...] = _rms(o_all, g_ref[...]).astype(o_ref.dtype)


def _mla(q, kv, proj, cc, ss, gain, *, batch, seq, kpe_col, tq=512):
    heads = MLA_HEADS
    t = q.shape[0]
    tq = min(tq, seq)
    nq = seq // tq
    width = heads * LANES
    return pl.pallas_call(
        functools.partial(_mla_kernel, heads=heads, scale=float((MLA_NOPE + MLA_ROPE) ** -0.5)),
        grid=(batch, nq),
        in_specs=[
            pl.BlockSpec((tq, q.shape[1]), lambda b, i: (b * nq + i, 0)),
            pl.BlockSpec((seq, width), lambda b, i: (b, 0)),
            pl.BlockSpec((seq, width), lambda b, i: (b, 1)),
            pl.BlockSpec((seq, LANES), lambda b, i: (b, kpe_col)),
            pl.BlockSpec((seq, LANES), lambda b, i: (b, kpe_col + 1)),
            pl.BlockSpec((seq, LANES), lambda b, i: (0, 0)),
            pl.BlockSpec((seq, LANES), lambda b, i: (0, 0)),
            pl.BlockSpec((tq, LANES), lambda b, i: (i, 0)),
            pl.BlockSpec((tq, LANES), lambda b, i: (i, 0)),
            pl.BlockSpec((1, width), lambda b, i: (0, 0)),
        ],
        out_specs=pl.BlockSpec((tq, width), lambda b, i: (b * nq + i, 0)),
        out_shape=jax.ShapeDtypeStruct((t, width), BF16),
        scratch_shapes=[
            pltpu.VMEM((heads, seq, 2 * LANES), BF16),
            pltpu.VMEM((2, tq, seq), F32),
            pltpu.VMEM((2, tq, seq), BF16),
        ],
        compiler_params=_params(("parallel", "arbitrary"), 56),
        name="mla_attention",
    )(q, kv, kv, proj, proj, cc, ss, cc, ss, gain.reshape(1, width).astype(F32))


NAT_ROWS_PER_STEP = 4


def _nat_kernel(q_ref, k_ref, v_ref, bias_ref, g_ref, o_ref, s_ref, p_ref, *, rows, pairs):
    band = NAT_WIN_R * GRID_W
    lane = lax.broadcasted_iota(jnp.int32, (GRID_W, LANES), 1)
    first = lane < NAT_HEAD_DIM
    for rr in range(NAT_ROWS_PER_STEP):
        r = pl.program_id(1) * NAT_ROWS_PER_STEP + rr
        qrows = slice(rr * GRID_W, (rr + 1) * GRID_W)
        rs = jnp.clip(r - NAT_WIN_R // 2, 0, rows - NAT_WIN_R)
        start = rs - r + (NAT_WIN_R - 1)
        koff = pl.multiple_of(rs * GRID_W, GRID_W)
        for j in range(pairs):
            cols = slice(j * LANES, (j + 1) * LANES)
            q2 = q_ref[qrows, cols] * jnp.asarray(NAT_HEAD_DIM ** -0.5, BF16)
            zero = jnp.zeros_like(q2)
            qq = jnp.concatenate([jnp.where(first, q2, zero), jnp.where(first, zero, q2)], axis=0)
            kb = k_ref[pl.ds(koff, band), cols]
            s_ref[rr, j] = lax.dot_general(qq, kb, (((1,), (1,)), ((), ())), preferred_element_type=F32)
        inv_l = []
        for j in range(pairs):
            bias = jnp.concatenate([bias_ref[j, start + 2 * m] for m in range(NAT_WIN_R // 2)], axis=1)
            s = s_ref[rr, j] + bias
            m = jnp.max(s, axis=-1, keepdims=True)
            p = jnp.exp(s - m)
            inv_l.append(1.0 / jnp.sum(p, axis=-1, keepdims=True))
            p_ref[rr, j] = p.astype(BF16)
        outs = []
        for j in range(pairs):
            cols = slice(j * LANES, (j + 1) * LANES)
            vb = v_ref[pl.ds(koff, band), cols]
            o = jnp.dot(p_ref[rr, j], vb, preferred_element_type=F32) * inv_l[j]
            outs.append(jnp.where(first, o[:GRID_W], o[GRID_W:]))
        o_all = jnp.concatenate(outs, axis=1)
        o_ref[qrows, :] = _rms(o_all, g_ref[...]).astype(o_ref.dtype)


def _nat(proj, bias_tbl, layer, gain, *, batch, seq):
    rows = seq // GRID_W
    assert rows >= NAT_WIN_R and rows % NAT_ROWS_PER_STEP == 0
    t = proj.shape[0]
    pairs, n_off = bias_tbl.shape[1], bias_tbl.shape[2]
    width = pairs * LANES
    steps = rows // NAT_ROWS_PER_STEP
    q_rows = NAT_ROWS_PER_STEP * GRID_W
    return pl.pallas_call(
        functools.partial(_nat_kernel, rows=rows, pairs=pairs),
        grid=(batch, steps),
        in_specs=[
            pl.BlockSpec((q_rows, width), lambda b, r: (b * steps + r, 0)),
            pl.BlockSpec((seq, width), lambda b, r: (b, 1)),
            pl.BlockSpec((seq, width), lambda b, r: (b, 2)),
            pl.BlockSpec((None, pairs, n_off, 2 * GRID_W, 2 * GRID_W), lambda b, r: (layer, 0, 0, 0, 0)),
            pl.BlockSpec((1, width), lambda b, r: (0, 0)),
        ],
        out_specs=pl.BlockSpec((q_rows, width), lambda b, r: (b * steps + r, 0)),
        out_shape=jax.ShapeDtypeStruct((t, width), BF16),
        scratch_shapes=[
            pltpu.VMEM((NAT_ROWS_PER_STEP, pairs, 2 * GRID_W, NAT_WIN_R * GRID_W), F32),
            pltpu.VMEM((NAT_ROWS_PER_STEP, pairs, 2 * GRID_W, NAT_WIN_R * GRID_W), BF16),
        ],
        compiler_params=_params(("parallel", "arbitrary"), 48),
        name="nat_attention",
    )(proj, proj, proj, bias_tbl, gain.reshape(1, width).astype(F32))


def _nat_bias_tables(rpb):
    n_l, heads, n_off = rpb.shape[0], rpb.shape[1], rpb.shape[2]
    qc = np.arange(GRID_W)[:, None]
    kc = np.arange(GRID_W)[None, :]
    c_start = np.clip(qc - NAT_WIN_C // 2, 0, GRID_W - NAT_WIN_C)
    valid = (kc >= c_start) & (kc < c_start + NAT_WIN_C)
    dc_idx = np.clip(kc - qc, -(NAT_WIN_C - 1), NAT_WIN_C - 1) + (NAT_WIN_C - 1)
    tbl = rpb.astype(F32)[:, :, :, dc_idx]
    tbl = jnp.where(jnp.asarray(valid), tbl, NEG_BIG)
    tbl = tbl.reshape(n_l, heads // 2, 2, n_off, GRID_W, GRID_W)
    tbl = jnp.transpose(tbl, (0, 1, 3, 2, 4, 5)).reshape(n_l, heads // 2, n_off, 2 * GRID_W, GRID_W)
    return jnp.concatenate([tbl[:, :, :-1], tbl[:, :, 1:]], axis=-1)


def _extract(x, rows, exact):
    m = jnp.max(x, axis=0, keepdims=True)
    hit = x == m
    if exact:
        first = jnp.min(jnp.where(hit, rows, float(x.shape[0])), axis=0, keepdims=True)
        hit = rows == first
    return m, hit


def _row_index(shape):
    return lax.broadcasted_iota(jnp.int32, shape, 0).astype(F32)


def _top_rows_ranked(x, dst_ref, k, exact):
    rows = _row_index(x.shape) if exact else None
    rank = jnp.full(x.shape, float(k), F32)
    for i in range(k):
        m, hit = _extract(x, rows, exact)
        dst_ref[i:i + 1, :] = m
        rank = jnp.where(hit, float(i), rank)
        x = jnp.where(hit, -jnp.inf, x)
    return rank, x


def _top2_rows(x1, x2, dst1_ref, dst2_ref, k, exact):
    rows = _row_index(x1.shape) if exact else None
    rank1 = jnp.full(x1.shape, float(k), F32) if exact else None
    rank2 = jnp.full(x2.shape, float(k), F32)
    for i in range(k):
        m1, hit1 = _extract(x1, rows, exact)
        m2, hit2 = _extract(x2, rows, exact)
        dst1_ref[i:i + 1, :] = m1
        dst2_ref[i:i + 1, :] = m2
        if exact:
            rank1 = jnp.where(hit1, float(i), rank1)
        x1 = jnp.where(hit1, -jnp.inf, x1)
        rank2 = jnp.where(hit2, float(i), rank2)
        x2 = jnp.where(hit2, -jnp.inf, x2)
    return rank1, rank2, x1, x2


def _count_removed(x):
    return jnp.sum(jnp.where(x == -jnp.inf, 1.0, 0.0), axis=0, keepdims=True)


def _dup_bf16_bits(x):
    bits = pltpu.bitcast(x.astype(BF16).astype(F32), jnp.uint32)
    return bits | (bits >> 16)


def _route_group(s1, s2, a_ref, b_ref, cand_ref, top_ref, exact):
    nk = s1.shape[0]
    rank1, rank2, x1, x2 = _top2_rows(s1, s2, a_ref, b_ref, PEER_TOPK, exact)
    offs, off = [], 0
    for r in range(PEER_TOPK):
        n_r = PEER_TOPK // (r + 1)
        cand_ref[off:off + n_r, :] = a_ref[r:r + 1, :] + b_ref[0:n_r, :]
        offs.append((off, n_r))
        off += n_r
    n_pad = cand_ref.shape[0] - off
    if n_pad:
        cand_ref[off:, :] = jnp.full((n_pad, LANES), -jnp.inf, F32)
    crank, xc = _top_rows_ranked(cand_ref[...], top_ref, PEER_TOPK, exact)
    taken = jnp.where(crank < float(PEER_TOPK), 1.0, 0.0)
    top = top_ref[...]
    z = jnp.sum(jnp.exp(top - top[0:1]), axis=0, keepdims=True)
    cnt = jnp.zeros((nk, LANES), F32)
    for r, (off, n_r) in enumerate(offs):
        cnt_r = jnp.sum(taken[off:off + n_r], axis=0, keepdims=True)
        is_r = (rank1 == float(r)) if exact else (s1 == a_ref[r:r + 1, :])
        cnt = jnp.where(is_r, cnt_r, cnt)
    c1 = jnp.exp(s1 - a_ref[0:1, :]) / z
    e2 = jnp.exp(s2 - b_ref[0:1, :])
    ties = None
    if not exact:
        k = float(PEER_TOPK)
        ties = ((_count_removed(x1) > k) | (_count_removed(x2) > k) | (_count_removed(xc) > k + n_pad))
    return cnt, c1, rank2, e2, ties


def _route_kernel(ht_ref, wq_ref, keys_ref, cnt_ref, c1_ref, r2_ref, e2_ref, a_ref, b_ref, cand_ref, top_ref):
    tm = ht_ref.shape[1]
    qt = jnp.dot(wq_ref[...], ht_ref[...], preferred_element_type=F32)
    half = qt.shape[0] // 2
    s1_all = jnp.dot(keys_ref[0], qt[:half].astype(BF16), preferred_element_type=F32)
    s2_all = jnp.dot(keys_ref[1], qt[half:].astype(BF16), preferred_element_type=F32)

    def store(c, cnt, c1, rank2, e2):
        lanes = slice(c * LANES, (c + 1) * LANES)
        cnt_ref[0, :, lanes] = _dup_bf16_bits(cnt)
        c1_ref[0, :, lanes] = _dup_bf16_bits(c1)
        r2_ref[0, :, lanes] = rank2.astype(BF16)
        e2_ref[0, :, lanes] = e2.astype(BF16)

    def group(c, exact):
        lanes = slice(c * LANES, (c + 1) * LANES)
        return _route_group(s1_all[:, lanes], s2_all[:, lanes], a_ref, b_ref, cand_ref, top_ref, exact)

    tie = []
    for c in range(tm // LANES):
        cnt, c1, rank2, e2, ties = group(c, exact=False)
        store(c, cnt, c1, rank2, e2)
        tie.append(jnp.where(ties, 1.0, 0.0))

    @pl.when(jnp.max(functools.reduce(jnp.maximum, tie)) > 0.0)
    def _():
        for c in range(tm // LANES):
            @pl.when(jnp.max(tie[c]) > 0.0)
            def _(c=c):
                store(c, *group(c, exact=True)[:4])


def _route(ht, wqt, keys, layer, *, tm=1024):
    d, t = ht.shape
    heads = PEER_HEADS
    dq2 = wqt.shape[1] // heads
    nk = keys.shape[2]
    tm = min(tm, t)
    n_cand = -(-sum(PEER_TOPK // (r + 1) for r in range(PEER_TOPK)) // 8) * 8
    row_shape = jax.ShapeDtypeStruct((heads, nk, t), jnp.uint32)
    row_spec = pl.BlockSpec((1, nk, tm), lambda i, h: (h, 0, i))
    tile_shape = jax.ShapeDtypeStruct((heads, nk, t), BF16)
    tile_spec = pl.BlockSpec((1, nk, tm), lambda i, h: (h, 0, i))
    return pl.pallas_call(
        _route_kernel,
        grid=(t // tm, heads),
        in_specs=[
            pl.BlockSpec((d, tm), lambda i, h: (0, i)),
            pl.BlockSpec((None, dq2, d), lambda i, h: (layer, h, 0)),
            pl.BlockSpec((None,) + keys.shape[1:], lambda i, h: (layer, 0, 0, 0)),
        ],
        out_specs=[row_spec, row_spec, tile_spec, tile_spec],
        out_shape=[row_shape, row_shape, tile_shape, tile_shape],
        scratch_shapes=[
            pltpu.VMEM((PEER_TOPK, LANES), F32),
            pltpu.VMEM((PEER_TOPK, LANES), F32),
            pltpu.VMEM((n_cand, LANES), F32),
            pltpu.VMEM((PEER_TOPK, LANES), F32),
        ],
        compiler_params=_params(("parallel", "arbitrary"), 40),
        name="peer_route",
    )(ht, wqt, keys)


def _gelu(x):
    return 0.5 * x * (1.0 + lax.erf(x * 0.7071067811865476))


GATE_LANES = 256
GATE_KGROUP = 2


def _row_as_bf16(row):
    return pltpu.bitcast(jnp.broadcast_to(row, (8, row.shape[1])), BF16)


def _build_gates(r2_ref, e2_ref, cnt_ref, c1_ref, row0, put, *, n1, heads):
    nk = r2_ref.shape[1]
    tm = r2_ref.shape[2]
    n_sub = nk // SUB16
    for k0 in range(0, n1, GATE_KGROUP):
        ks = range(k0, min(k0 + GATE_KGROUP, n1))
        for c in range(tm // GATE_LANES):
            lanes = slice(c * GATE_LANES, (c + 1) * GATE_LANES)
            g = {(k, s): jnp.zeros((SUB16, GATE_LANES), BF16) for k in ks for s in range(n_sub)}
            for h in range(heads):
                cnt = {k: _row_as_bf16(cnt_ref[h, row0 + k:row0 + k + 1, lanes]) for k in ks}
                c1 = {k: _row_as_bf16(c1_ref[h, row0 + k:row0 + k + 1, lanes]) for k in ks}
                for s in range(n_sub):
                    rows = slice(s * SUB16, (s + 1) * SUB16)
                    r2 = r2_ref[h, rows, lanes]
                    e2 = e2_ref[h, rows, lanes]
                    for k in ks:
                        g[k, s] = g[k, s] + jnp.where(r2 < cnt[k], e2, jnp.zeros_like(e2)) * c1[k]
            for k in ks:
                for s in range(n_sub):
                    put(k, s, c, g[k, s])


def _experts_kernel(ht_ref, u_ref, vt_ref, cnt_ref, c1_ref, cntn_ref, c1n_ref, r2_ref, e2_ref, x_ref, gf_ref,
                    o_ref, acc_ref, ga_ref, a_ref, *, n1, heads, final_norm):
    j = pl.program_id(1)
    nk = r2_ref.shape[1]
    tm = ht_ref.shape[1]
    te = n1 * nk
    n_sub = nk // SUB16
    build = functools.partial(_build_gates, r2_ref, e2_ref, n1=n1, heads=heads)
    slot, next_slot = j % 2, (j + 1) % 2

    def put_slot(which):
        def put(k, s, c, tile):
            ga_ref[which, k * nk + s * SUB16:k * nk + (s + 1) * SUB16, c * GATE_LANES:(c + 1) * GATE_LANES] = tile
        return put

    @pl.when(j == 0)
    def _():
        acc_ref[...] = jnp.zeros_like(acc_ref)
        build(cnt_ref, c1_ref, 0, put_slot(0))

    act0 = jnp.dot(u_ref[0:te, :], ht_ref[...], preferred_element_type=F32)
    a_ref[0:te, :] = _gelu(act0).astype(BF16) * ga_ref[slot]
    build(cntn_ref, c1n_ref, 0, put_slot(next_slot))
    tiles = {}
    build(cnt_ref, c1_ref, n1, lambda k, s, c, tile: tiles.__setitem__((k, s, c), tile))
    gb = jnp.concatenate(
        [jnp.concatenate([tiles[k, s, c] for c in range(tm // GATE_LANES)], axis=1)
         for k in range(n1) for s in range(n_sub)], axis=0)
    act1 = jnp.dot(u_ref[te:2 * te, :], ht_ref[...], preferred_element_type=F32)
    a_ref[te:2 * te, :] = _gelu(act1).astype(BF16) * gb
    acc_ref[...] += jnp.dot(vt_ref[...], a_ref[...], preferred_element_type=F32)

    @pl.when(j == pl.num_programs(1) - 1)
    def _():
        y = x_ref[...] + acc_ref[...].T
        o_ref[...] = _rms(y, gf_ref[...]) if final_norm else y


def _experts(ht, u, vt, layer, cnt, c1, r2, e2, x, gain_final=None, *, tm=512, n1=4):
    d, t = ht.shape
    heads, nk, _ = cnt.shape
    n_e = u.shape[1]
    tm = min(tm, t)
    te = n1 * nk
    n_j = n_e // (2 * te)
    assert n_e == n_j * 2 * te and (2 * n1) % 8 == 0
    rows_now = pl.BlockSpec((heads, 2 * n1, tm), lambda i, j: (0, j, i))
    rows_next = pl.BlockSpec((heads, 2 * n1, tm), lambda i, j: (0, jnp.minimum(j + 1, n_j - 1), i))
    tile_spec = pl.BlockSpec((heads, nk, tm), lambda i, j: (0, 0, i))
    return pl.pallas_call(
        functools.partial(_experts_kernel, n1=n1, heads=heads, final_norm=gain_final is not None),
        grid=(t // tm, n_j),
        in_specs=[
            pl.BlockSpec((d, tm), lambda i, j: (0, i)),
            pl.BlockSpec((None, 2 * te, d), lambda i, j: (layer, j, 0)),
            pl.BlockSpec((None, d, 2 * te), lambda i, j: (layer, 0, j)),
            rows_now, rows_now, rows_next, rows_next, tile_spec, tile_spec,
            pl.BlockSpec((tm, d), lambda i, j: (i, 0)),
            pl.BlockSpec((1, d), lambda i, j: (0, 0)),
        ],
        out_specs=pl.BlockSpec((tm, d), lambda i, j: (i, 0)),
        out_shape=jax.ShapeDtypeStruct((t, d), F32),
        scratch_shapes=[
            pltpu.VMEM((d, tm), F32),
            pltpu.VMEM((2, te, tm), BF16),
            pltpu.VMEM((2 * te, tm), BF16),
        ],
        compiler_params=_params(("parallel", "arbitrary"), 56),
        name="peer_experts",
    )(ht, u, vt, cnt, c1, cnt, c1, r2, e2, x,
      (jnp.ones((d,), F32) if gain_final is None else gain_final).reshape(1, d).astype(F32))


def _swap_halves(w):
    half = w.shape[-1] // 2
    return jnp.concatenate([w[..., half:], w[..., :half]], axis=-1)


def _prep_w_in(w_in, q_lora, kv_lora, nat_width):
    o1 = q_lora
    o2 = o1 + kv_lora
    o3 = o2 + MLA_ROPE
    w_in = w_in.astype(BF16)
    c_q, c_kv, k_pe, nat = w_in[..., :o1], w_in[..., o1:o2], w_in[..., o2:o3], w_in[..., o3:]
    assert nat.shape[-1] == 3 * nat_width
    k_sw = _swap_halves(k_pe)
    return jnp.concatenate([nat, c_q, c_kv, k_pe, k_pe, k_sw, k_sw], axis=-1)


def _prep_w_uq(w_uq):
    n_l, k, _ = w_uq.shape
    w = w_uq.reshape(n_l, k, MLA_HEADS, MLA_NOPE + MLA_ROPE)
    nope = w[..., :MLA_NOPE].reshape(n_l, k, MLA_HEADS * MLA_NOPE)
    pe = w[..., MLA_NOPE:]
    return jnp.concatenate([nope, pe.reshape(n_l, k, -1), _swap_halves(pe).reshape(n_l, k, -1)], axis=-1).astype(BF16)


def _prep_w_ukv(w_ukv):
    n_l, k, n = w_ukv.shape
    w = w_ukv.reshape(n_l, k, MLA_HEADS, n // MLA_HEADS)
    return jnp.concatenate([w[..., :MLA_NOPE].reshape(n_l, k, -1), w[..., MLA_NOPE:].reshape(n_l, k, -1)],
                           axis=-1).astype(BF16)


def _rope_tables(seq):
    inv = ROPE_BASE ** (-jnp.arange(0, MLA_ROPE, 2, dtype=F32) / MLA_ROPE)
    ang = jnp.arange(seq, dtype=F32)[:, None] * inv[None, :]
    cos, sin = jnp.cos(ang), jnp.sin(ang)
    return jnp.concatenate([cos, cos, cos, cos], axis=-1), jnp.concatenate([-sin, sin, -sin, sin], axis=-1)


def kernel(x, attn_norm, w_in, mla_q_norm, mla_w_uq, mla_kv_norm, mla_w_ukv, nat_rpb, mla_out_norm, nat_out_norm,
           w_out, ffn_norm, peer_w_q, peer_sub_keys, peer_u, peer_v, final_norm):
    batch, seq, d = x.shape
    depth = w_in.shape[0]
    q_lora, kv_lora = mla_q_norm.shape[1], mla_kv_norm.shape[1]
    nat_width = nat_rpb.shape[1] * NAT_HEAD_DIM
    mla_width = mla_out_norm.shape[1]
    assert mla_w_ukv.shape[2] == MLA_HEADS * 2 * LANES and mla_width == MLA_HEADS * LANES
    assert nat_width == mla_width and q_lora % LANES == 0 and kv_lora % LANES == 0
    t = batch * seq

    w1 = _prep_w_in(w_in, q_lora, kv_lora, nat_width)
    cq_col = 3 * nat_width // q_lora
    ckv_col = (3 * nat_width + q_lora) // kv_lora
    kpe_col = (3 * nat_width + q_lora + kv_lora) // LANES
    wq = _prep_w_uq(mla_w_uq)
    wkv = _prep_w_ukv(mla_w_ukv)
    wo = w_out.astype(BF16)
    wpq_t = jnp.swapaxes(peer_w_q, 1, 2).astype(BF16)
    keys = peer_sub_keys.astype(BF16)
    u = peer_u.astype(BF16)
    vt = jnp.swapaxes(peer_v, 1, 2).astype(BF16)
    bias_tbl = _nat_bias_tables(nat_rpb)
    cc, ss = _rope_tables(seq)

    xf = x.reshape(t, d)
    for l in range(depth):
        proj = _matmul([(xf, 0, d)], w1, l, out_dtype=BF16, gain=attn_norm[l], tn=2048, name="proj_in")
        q = _matmul([(proj, cq_col, q_lora)], wq, l, out_dtype=BF16, gain=mla_q_norm[l], tn=2048, name="mla_q_up")
        kv = _matmul([(proj, ckv_col, kv_lora)], wkv, l, out_dtype=BF16, gain=mla_kv_norm[l], tn=2048,
                     name="mla_kv_up")
        mla_o = _mla(q, kv, proj, cc, ss, mla_out_norm[l], batch=batch, seq=seq, kpe_col=kpe_col)
        nat_o = _nat(proj, bias_tbl, l, nat_out_norm[l], batch=batch, seq=seq)
        xf, ht = _matmul([(mla_o, 0, mla_width), (nat_o, 0, nat_width)], wo, l, out_dtype=F32, res=xf,
                         gain_out=ffn_norm[l], tn=d, name="mix_out")
        cnt, c1, r2, e2 = _route(ht, wpq_t, keys, l)
        xf = _experts(ht, u, vt, l, cnt, c1, r2, e2, xf, final_norm if l == depth - 1 else None)
    return xf.reshape(batch, seq, d)
```
